```python
import jax, jax.numpy as jnp
from jax import lax
import numpy as np

D_MODEL = 1024
BATCH = 8
SEQ = 4096
DEPTH = 2

D_MIX = 512
GMLP_GROUPS = 8
GMLP_CHUNK = 128
GMLP_GROUP_DIM = D_MIX // GMLP_GROUPS
CONV_WIDTH = 31
FOX_HEADS = 8
FOX_HEAD_DIM = D_MIX // FOX_HEADS
QBLK = 128
N_BRANCH = 3
D_FF = 4 * D_MODEL
N_MOD = 6
NORM_EPS = 1e-6

COL_GMLP = 2 * D_MIX
COL_CONV = 2 * D_MIX
COL_FOX = 3 * D_MIX + FOX_HEADS
COL_GATE = N_BRANCH * D_MODEL
D_IN = COL_GMLP + COL_CONV + COL_FOX + COL_GATE
SPLIT_IDX = (COL_GMLP,
             COL_GMLP + COL_CONV,
             COL_GMLP + COL_CONV + D_MIX,
             COL_GMLP + COL_CONV + 2 * D_MIX,
             COL_GMLP + COL_CONV + 3 * D_MIX,
             COL_GMLP + COL_CONV + 3 * D_MIX + FOX_HEADS)

kernel_name = "hybrid_gmlp_conformer_fox_sandwich_adaln"


def rms_norm(x, g):
    xf = x.astype(jnp.float32)
    y = xf * lax.rsqrt(jnp.mean(xf * xf, axis=-1, keepdims=True) + NORM_EPS)
    return (y * g.astype(jnp.float32)).astype(x.dtype)


def layer_norm(x, g, b):
    xf = x.astype(jnp.float32)
    mu = jnp.mean(xf, axis=-1, keepdims=True)
    var = jnp.mean(jnp.square(xf - mu), axis=-1, keepdims=True)
    y = (xf - mu) * lax.rsqrt(var + NORM_EPS)
    return (y * g.astype(jnp.float32) + b.astype(jnp.float32)).astype(x.dtype)


def gmlp_spatial(v, ws, bs):
    b, s, _ = v.shape
    nc = s // GMLP_CHUNK
    vc = v.reshape(b, nc, GMLP_CHUNK, GMLP_GROUPS, GMLP_GROUP_DIM)
    w = ws * jnp.tril(jnp.ones((GMLP_CHUNK, GMLP_CHUNK), ws.dtype))
    sv = jnp.einsum('gts,bnsgc->bntgc', w, vc) + bs.T[None, None, :, :, None]
    return sv.reshape(b, s, D_MIX)


def causal_depthwise_conv(z, w, bias):
    out = lax.conv_general_dilated(
        z, w[:, None, :], window_strides=(1,), padding=[(CONV_WIDTH - 1, 0)],
        dimension_numbers=('NWC', 'WIO', 'NWC'), feature_group_count=D_MIX)
    return out + bias


def fox_attention(q, k, v, log_f):
    b, h, s, dh = q.shape
    nb = s // QBLK
    cum = jnp.cumsum(log_f.astype(jnp.float32), axis=-1)
    qb = jnp.moveaxis(q.reshape(b, h, nb, QBLK, dh), 2, 0)
    cq = jnp.moveaxis(cum.reshape(b, h, nb, QBLK), 2, 0)
    kpos = jnp.arange(s)
    scale = 1.0 / float(np.sqrt(dh))

    def one_block(args):
        qi, ci, i = args
        logits = jnp.einsum('bhqd,bhkd->bhqk', qi, k).astype(jnp.float32) * scale
        logits = logits + ci[..., None] - cum[:, :, None, :]
        qpos = i * QBLK + jnp.arange(QBLK)
        causal = kpos[None, :] <= qpos[:, None]
        logits = jnp.where(causal, logits, -jnp.inf)
        p = jax.nn.softmax(logits, axis=-1)
        return jnp.einsum('bhqk,bhkd->bhqd', p.astype(v.dtype), v)

    out = lax.map(one_block, (qb, cq, jnp.arange(nb)))
    return out.transpose(1, 0, 3, 2, 4).reshape(b, s, h * dh)


def hybrid_mixer(h, w_in, gmlp_ln_g, gmlp_ln_b, gmlp_ws, gmlp_bs, w_a_out,
                 conv_w, conv_b, conv_ln_g, conv_ln_b, w_b_out, fox_bf, w_c_out, w_out):
    b, s, _ = h.shape
    proj = h @ w_in
    uv_a, glu_b, q, k, v, f_raw, gate_raw = jnp.split(proj, SPLIT_IDX, axis=-1)

    u_a, v_a = jnp.split(jax.nn.gelu(uv_a), 2, axis=-1)
    v_a = layer_norm(v_a, gmlp_ln_g, gmlp_ln_b)
    y_a = (u_a * gmlp_spatial(v_a, gmlp_ws, gmlp_bs)) @ w_a_out

    val_b, gat_b = jnp.split(glu_b, 2, axis=-1)
    z = val_b * jax.nn.sigmoid(gat_b)
    z = causal_depthwise_conv(z, conv_w, conv_b)
    z = jax.nn.silu(layer_norm(z, conv_ln_g, conv_ln_b))
    y_b = z @ w_b_out

    def heads(t):
        return t.reshape(b, s, FOX_HEADS, FOX_HEAD_DIM).transpose(0, 2, 1, 3)
    log_f = jax.nn.log_sigmoid((f_raw + fox_bf).astype(jnp.float32)).transpose(0, 2, 1)
    y_c = fox_attention(heads(q), heads(k), heads(v), log_f) @ w_c_out

    g = jax.nn.sigmoid(gate_raw).reshape(b, s, N_BRANCH, D_MODEL)
    merged = g[:, :, 0] * y_a + g[:, :, 1] * y_b + g[:, :, 2] * y_c
    return merged @ w_out


def setup_inputs(seed: int = 0) -> dict:
    key = jax.random.key(seed)
    ks = jax.random.split(key, 32)
    f32 = jnp.float32

    def nrm(k, shape, scale):
        return jax.random.normal(k, shape, f32) * scale

    L = DEPTH
    return {
        "x": nrm(ks[0], (BATCH, SEQ, D_MODEL), 1.0),
        "c": nrm(ks[1], (BATCH, D_MODEL), 1.0),
        "ada_w": nrm(ks[2], (L, D_MODEL, N_MOD * D_MODEL), 0.5 * D_MODEL ** -0.5),
        "ada_b": nrm(ks[3], (L, N_MOD * D_MODEL), 0.02),
        "mix_pre_g": 1.0 + nrm(ks[4], (L, D_MODEL), 0.02),
        "mix_post_g": 1.0 + nrm(ks[5], (L, D_MODEL), 0.02),
        "mlp_pre_g": 1.0 + nrm(ks[6], (L, D_MODEL), 0.02),
        "mlp_post_g": 1.0 + nrm(ks[7], (L, D_MODEL), 0.02),
        "w_in": nrm(ks[8], (L, D_MODEL, D_IN), D_MODEL ** -0.5),
        "gmlp_ln_g": 1.0 + nrm(ks[9], (L, D_MIX), 0.02),
        "gmlp_ln_b": nrm(ks[10], (L, D_MIX), 0.02),
        "gmlp_ws": nrm(ks[11], (L, GMLP_GROUPS, GMLP_CHUNK, GMLP_CHUNK), GMLP_CHUNK ** -0.5),
        "gmlp_bs": 1.0 + nrm(ks[12], (L, GMLP_GROUPS, GMLP_CHUNK), 0.02),
        "w_a_out": nrm(ks[13], (L, D_MIX, D_MODEL), D_MIX ** -0.5),
        "conv_w": nrm(ks[14], (L, CONV_WIDTH, D_MIX), CONV_WIDTH ** -0.5),
        "conv_b": nrm(ks[15], (L, D_MIX), 0.02),
        "conv_ln_g": 1.0 + nrm(ks[16], (L, D_MIX), 0.02),
        "conv_ln_b": nrm(ks[17], (L, D_MIX), 0.02),
        "w_b_out": nrm(ks[18], (L, D_MIX, D_MODEL), D_MIX ** -0.5),
        "fox_bf": jnp.linspace(1.0, 6.0, FOX_HEADS, dtype=f32)[None, :] + nrm(ks[19], (L, FOX_HEADS), 0.1),
        "w_c_out": nrm(ks[20], (L, D_MIX, D_MODEL), D_MIX ** -0.5),
        "w_out": nrm(ks[21], (L, D_MODEL, D_MODEL), D_MODEL ** -0.5),
        "mlp_w1": nrm(ks[22], (L, D_MODEL, D_FF), D_MODEL ** -0.5),
        "mlp_w2": nrm(ks[23], (L, D_FF, D_MODEL), D_FF ** -0.5),
    }


def reference(x, c, ada_w, ada_b, mix_pre_g, mix_post_g, mlp_pre_g, mlp_post_g, w_in,
              gmlp_ln_g, gmlp_ln_b, gmlp_ws, gmlp_bs, w_a_out, conv_w, conv_b, conv_ln_g,
              conv_ln_b, w_b_out, fox_bf, w_c_out, w_out, mlp_w1, mlp_w2):
    c_act = jax.nn.silu(c)
    for l in range(DEPTH):
        mod = (c_act @ ada_w[l] + ada_b[l])[:, None, :]
        sh1, sc1, gt1, sh2, sc2, gt2 = jnp.split(mod, N_MOD, axis=-1)

        h = rms_norm(x, mix_pre_g[l]) * (1.0 + sc1) + sh1
        y = hybrid_mixer(h, w_in[l], gmlp_ln_g[l], gmlp_ln_b[l], gmlp_ws[l], gmlp_bs[l],
                         w_a_out[l], conv_w[l], conv_b[l], conv_ln_g[l], conv_ln_b[l],
                         w_b_out[l], fox_bf[l], w_c_out[l], w_out[l])
        x = x + gt1 * rms_norm(y, mix_post_g[l])

        h = rms_norm(x, mlp_pre_g[l]) * (1.0 + sc2) + sh2
        y = jnp.square(jax.nn.relu(h @ mlp_w1[l])) @ mlp_w2[l]
        x = x + gt2 * rms_norm(y, mlp_post_g[l])
    return x
```

```python
import functools
import math

import jax
import jax.numpy as jnp
from jax import lax
from jax.experimental import pallas as pl
from jax.experimental.pallas import tpu as pltpu

D_MODEL = 1024
D_MIX = 512
N_GROUPS = 8
CHUNK = 128
GROUP_DIM = D_MIX // N_GROUPS
CONV_WIDTH = 31
N_HEADS = 8
HEAD_DIM = D_MIX // N_HEADS
N_BRANCH = 3
D_FF = 4 * D_MODEL
N_MOD = 6
NORM_EPS = 1e-6

LANES = 128
N_PAIRS = D_MIX // LANES
CONV_HALO = 32
CUM_CHUNK = 256
VMEM_LIMIT = 56 * 1024 * 1024

ROW_TILE = 512
ATTN_TILE = 512

F32 = jnp.float32
BF16 = jnp.bfloat16


def _dot(a, b):
    return jnp.dot(a, b, preferred_element_type=F32)


def _sigmoid(x):
    return 0.5 * jnp.tanh(0.5 * x) + 0.5


def _gelu_tanh(x):
    c = math.sqrt(2.0 / math.pi)
    return 0.5 * x * (1.0 + jnp.tanh(c * (x + 0.044715 * (x * x * x))))


def _rms(x):
    return x * lax.rsqrt(jnp.mean(x * x, axis=-1, keepdims=True) + NORM_EPS)


def _layer_norm(x, g, b):
    mu = jnp.mean(x, axis=-1, keepdims=True)
    xc = x - mu
    var = jnp.mean(xc * xc, axis=-1, keepdims=True)
    return xc * lax.rsqrt(var + NORM_EPS) * g + b


def _split3(x):
    hi = x.astype(BF16)
    r = x - hi.astype(F32)
    mid = r.astype(BF16)
    lo = (r - mid.astype(F32)).astype(BF16)
    return hi, mid, lo


def _ada_kernel(c_ref, w_ref, b_ref, o_ref):
    c = c_ref[...]
    c_act = (c * _sigmoid(c)).astype(BF16)
    o_ref[0] = _dot(c_act, w_ref[0].astype(BF16)) + b_ref[0]


def _ada_call(c, ada_w, ada_b):
    n_layers, d, n_out = ada_w.shape
    b = c.shape[0]
    tn = 2 * D_MODEL
    return pl.pallas_call(
        _ada_kernel,
        grid=(n_layers, n_out // tn),
        in_specs=[
            pl.BlockSpec((b, d), lambda l, j: (0, 0)),
            pl.BlockSpec((1, d, tn), lambda l, j: (l, 0, j)),
            pl.BlockSpec((1, 1, tn), lambda l, j: (l, 0, j)),
        ],
        out_specs=pl.BlockSpec((1, b, tn), lambda l, j: (l, 0, j)),
        out_shape=jax.ShapeDtypeStruct((n_layers, b, n_out), F32),
        compiler_params=pltpu.CompilerParams(
            dimension_semantics=("arbitrary", "arbitrary"), vmem_limit_bytes=VMEM_LIMIT),
        name="ada_mod",
    )(c, ada_w, ada_b.reshape(n_layers, 1, n_out))


def _mixer_in_kernel(x_ref, mod_ref, g_ref, wuv_ref, wglu_ref, wqkv_ref, wf_ref,
                     lng_ref, lnb_ref, wsp_ref, bsp_ref, cw_ref, cb_ref, clg_ref, clb_ref,
                     bf_ref,
                     ya_ref, zb_ref, qkv_ref, fcol_ref, frow_ref,
                     zs_ref, carry_ref):
    tm = x_ref.shape[1]
    s_idx = pl.program_id(1)

    @pl.when(s_idx == 0)
    def _():
        zs_ref[0:CONV_HALO, :] = jnp.zeros((CONV_HALO, D_MIX), F32)
        carry_ref[...] = jnp.zeros_like(carry_ref)

    x = x_ref[0]
    mod = mod_ref[0, 0]
    sh1 = mod[0:1]
    sc1 = mod[1:2]
    h = (_rms(x) * (g_ref[...] * (1.0 + sc1)) + sh1).astype(BF16)

    uv = _gelu_tanh(_dot(h, wuv_ref[...]))
    u_a = uv[:, :D_MIX]
    v_a = _layer_norm(uv[:, D_MIX:], lng_ref[...], lnb_ref[...]).astype(BF16)
    n_chunks = tm // CHUNK
    lane = lax.broadcasted_iota(jnp.int32, (CHUNK, LANES), 1)
    w_row = lax.broadcasted_iota(jnp.int32, (CHUNK, 2 * CHUNK), 0)
    w_col = lax.broadcasted_iota(jnp.int32, (CHUNK, 2 * CHUNK), 1)
    causal_w = (w_col % CHUNK) <= w_row
    zero_bf = jnp.zeros((CHUNK, LANES), BF16)
    for p in range(N_PAIRS):
        cols = slice(p * LANES, (p + 1) * LANES)
        w_pair = jnp.where(causal_w, wsp_ref[p], jnp.zeros_like(wsp_ref[p]))
        rhs = []
        for c in range(n_chunks):
            vc = v_a[c * CHUNK:(c + 1) * CHUNK, cols]
            rhs.append(jnp.concatenate(
                [jnp.where(lane < GROUP_DIM, vc, zero_bf),
                 jnp.where(lane >= GROUP_DIM, vc, zero_bf)], axis=0))
        sv = _dot(w_pair, jnp.concatenate(rhs, axis=1))
        bias = bsp_ref[:, cols]
        for c in range(n_chunks):
            rows = slice(c * CHUNK, (c + 1) * CHUNK)
            ya_ref[0, rows, cols] = (
                u_a[rows, cols] * (sv[:, c * LANES:(c + 1) * LANES] + bias)).astype(BF16)

    glu = _dot(h, wglu_ref[...])
    zs_ref[CONV_HALO:CONV_HALO + tm, :] = glu[:, :D_MIX] * _sigmoid(glu[:, D_MIX:])
    first = CONV_HALO - (CONV_WIDTH - 1)
    acc = jnp.broadcast_to(cb_ref[...], (tm, D_MIX))
    for j in range(CONV_WIDTH):
        acc = acc + cw_ref[j:j + 1, :] * zs_ref[first + j:first + j + tm, :]
    zs_ref[0:CONV_HALO, :] = zs_ref[tm:tm + CONV_HALO, :]
    zn = _layer_norm(acc, clg_ref[...], clb_ref[...])
    zb_ref[0] = (zn * _sigmoid(zn)).astype(BF16)

    qkv = _dot(h, wqkv_ref[...])
    scale = 1.0 / math.sqrt(HEAD_DIM)
    qkv_ref[0, :, 0:D_MIX] = (qkv[:, 0:D_MIX] * scale).astype(BF16)
    qkv_ref[0, :, D_MIX:] = qkv[:, D_MIX:].astype(BF16)

    f_raw = _dot(h, wf_ref[...]) + bf_ref[...]
    log_f = jnp.minimum(f_raw, 0.0) - jnp.log1p(jnp.exp(-jnp.abs(f_raw)))
    t_row = lax.broadcasted_iota(jnp.int32, (CUM_CHUNK, CUM_CHUNK), 0)
    t_col = lax.broadcasted_iota(jnp.int32, (CUM_CHUNK, CUM_CHUNK), 1)
    tri = jnp.where(t_col <= t_row, 1.0, 0.0).astype(BF16)
    carry = carry_ref[...]
    for c in range(tm // CUM_CHUNK):
        rows = slice(c * CUM_CHUNK, (c + 1) * CUM_CHUNK)
        hi, mid, lo = _split3(log_f[rows])
        cs = _dot(tri, hi) + _dot(tri, mid) + _dot(tri, lo) + carry
        carry = cs[CUM_CHUNK - 1:CUM_CHUNK, :]
        fcol_ref[0, rows, :] = cs
        frow_ref[0, :, rows] = cs.T[0:N_HEADS, :]
    carry_ref[...] = carry


def _mixer_in_call(x, mod, g, wuv, wglu, wqkv, wf, lng, lnb, wsp, bsp, cw, cb, clg, clb, bf, layer):
    b, s, d = x.shape
    tm = ROW_TILE
    const2 = lambda bi, si: (0, 0)
    const3 = lambda bi, si: (0, 0, 0)
    row3 = lambda bi, si: (bi, si, 0)
    out_shapes = (
        jax.ShapeDtypeStruct((b, s, D_MIX), BF16),
        jax.ShapeDtypeStruct((b, s, D_MIX), BF16),
        jax.ShapeDtypeStruct((b, s, 3 * D_MIX), BF16),
        jax.ShapeDtypeStruct((b, s, LANES), F32),
        jax.ShapeDtypeStruct((b, N_HEADS, s), F32),
    )
    return pl.pallas_call(
        _mixer_in_kernel,
        grid=(b, s // tm),
        in_specs=[
            pl.BlockSpec((1, tm, d), row3),
            pl.BlockSpec((1, 1, N_MOD, d), lambda bi, si: (layer, bi, 0, 0)),
            pl.BlockSpec((1, d), const2),
            pl.BlockSpec((d, 2 * D_MIX), const2),
            pl.BlockSpec((d, 2 * D_MIX), const2),
            pl.BlockSpec((d, 3 * D_MIX), const2),
            pl.BlockSpec((d, LANES), const2),
            pl.BlockSpec((1, D_MIX), const2),
            pl.BlockSpec((1, D_MIX), const2),
            pl.BlockSpec((N_PAIRS, CHUNK, 2 * CHUNK), const3),
            pl.BlockSpec((CHUNK, D_MIX), const2),
            pl.BlockSpec((CONV_HALO, D_MIX), const2),
            pl.BlockSpec((1, D_MIX), const2),
            pl.BlockSpec((1, D_MIX), const2),
            pl.BlockSpec((1, D_MIX), const2),
            pl.BlockSpec((1, LANES), const2),
        ],
        out_specs=(
            pl.BlockSpec((1, tm, D_MIX), row3),
            pl.BlockSpec((1, tm, D_MIX), row3),
            pl.BlockSpec((1, tm, 3 * D_MIX), row3),
            pl.BlockSpec((1, tm, LANES), row3),
            pl.BlockSpec((1, N_HEADS, tm), lambda bi, si: (bi, 0, si)),
        ),
        out_shape=out_shapes,
        scratch_shapes=[
            pltpu.VMEM((CONV_HALO + tm, D_MIX), F32),
            pltpu.VMEM((1, LANES), F32),
        ],
        compiler_params=pltpu.CompilerParams(
            dimension_semantics=("arbitrary", "arbitrary"), vmem_limit_bytes=VMEM_LIMIT),
        name="mixer_in",
    )(x, mod, g, wuv, wglu, wqkv, wf, lng, lnb, wsp, bsp, cw, cb, clg, clb, bf)


def _attn_kernel(q_ref, k_ref, v_ref, fcol_ref, fra_ref, frb_ref, o_ref,
                 m_ref, l_ref, acc_ref):
    tq = q_ref.shape[1]
    tk = tq
    pair = pl.program_id(1)
    qi = pl.program_id(2)

    lane = lax.broadcasted_iota(jnp.int32, (tq, LANES), 1)
    lo_lanes = lane < HEAD_DIM
    q = q_ref[0]
    q_heads = (jnp.where(lo_lanes, q, jnp.zeros_like(q)),
               jnp.where(lo_lanes, jnp.zeros_like(q), q))
    fcol = fcol_ref[0]
    f_q = tuple(jnp.sum(jnp.where(lane == 2 * pair + hh, fcol, 0.0), axis=1, keepdims=True)
                for hh in range(2))
    f_k_refs = (fra_ref, frb_ref)

    m_ref[...] = jnp.full(m_ref.shape, -1e30, F32)
    l_ref[...] = jnp.zeros_like(l_ref)
    acc_ref[...] = jnp.zeros_like(acc_ref)

    def step(j, masked):
        start = pl.multiple_of(j * tk, tk)
        kj = k_ref[0, pl.ds(start, tk), :]
        vj = v_ref[0, pl.ds(start, tk), :]
        pv = []
        alpha = []
        for hh in range(2):
            s = lax.dot_general(q_heads[hh], kj, (((1,), (1,)), ((), ())),
                                preferred_element_type=F32)
            s = s + (f_q[hh] - f_k_refs[hh][0, :, pl.ds(start, tk)])
            if masked:
                r = lax.broadcasted_iota(jnp.int32, (tq, tk), 0)
                c = lax.broadcasted_iota(jnp.int32, (tq, tk), 1)
                s = jnp.where(c <= r, s, -jnp.inf)
            m_old = m_ref[hh]
            m_new = jnp.maximum(m_old, jnp.max(s, axis=1, keepdims=True))
            a = jnp.exp(m_old - m_new)
            p = jnp.exp(s - m_new)
            l_ref[hh] = a * l_ref[hh] + jnp.sum(p, axis=1, keepdims=True)
            m_ref[hh] = m_new
            pv.append(_dot(p.astype(BF16), vj))
            alpha.append(a)
        acc_ref[...] = (jnp.where(lo_lanes, alpha[0], alpha[1]) * acc_ref[...]
                        + jnp.where(lo_lanes, pv[0], pv[1]))

    def body(j, carry):
        step(j, masked=False)
        return carry

    lax.fori_loop(0, qi, body, 0)
    step(qi, masked=True)

    inv_l = jnp.where(lo_lanes, 1.0 / l_ref[0], 1.0 / l_ref[1])
    o_ref[0] = (acc_ref[...] * inv_l).astype(BF16)


def _attn_call(qkv, fcol, frow):
    b, s, _ = qkv.shape
    tq = ATTN_TILE
    frow2 = frow.reshape(b * N_HEADS, 1, s)
    return pl.pallas_call(
        _attn_kernel,
        grid=(b, N_PAIRS, s // tq),
        in_specs=[
            pl.BlockSpec((1, tq, LANES), lambda bi, p, i: (bi, i, p)),
            pl.BlockSpec((1, s, LANES), lambda bi, p, i: (bi, 0, N_PAIRS + p)),
            pl.BlockSpec((1, s, LANES), lambda bi, p, i: (bi, 0, 2 * N_PAIRS + p)),
            pl.BlockSpec((1, tq, LANES), lambda bi, p, i: (bi, i, 0)),
            pl.BlockSpec((1, 1, s), lambda bi, p, i: (bi * N_HEADS + 2 * p, 0, 0)),
            pl.BlockSpec((1, 1, s), lambda bi, p, i: (bi * N_HEADS + 2 * p + 1, 0, 0)),
        ],
        out_specs=pl.BlockSpec((1, tq, LANES), lambda bi, p, i: (bi, i, p)),
        out_shape=jax.ShapeDtypeStruct((b, s, D_MIX), BF16),
        scratch_shapes=[
            pltpu.VMEM((2, tq, 1), F32),
            pltpu.VMEM((2, tq, 1), F32),
            pltpu.VMEM((tq, LANES), F32),
        ],
        compiler_params=pltpu.CompilerParams(
            dimension_semantics=("arbitrary", "arbitrary", "arbitrary"),
            vmem_limit_bytes=VMEM_LIMIT),
        name="fox_attn",
    )(qkv, qkv, qkv, fcol, frow2, frow2)


def _merge_kernel(x_ref, mod_ref, gpre_ref, gpost_ref, ya_ref, zb_ref, yc_ref,
                  wgate_ref, wa_ref, wb_ref, wc_ref, wout_ref, o_ref):
    x = x_ref[0]
    mod = mod_ref[0, 0]
    sh1 = mod[0:1]
    sc1 = mod[1:2]
    gt1 = mod[2:3]
    h = (_rms(x) * (gpre_ref[...] * (1.0 + sc1)) + sh1).astype(BF16)
    merged = None
    for br, (in_ref, w_ref) in enumerate(((ya_ref, wa_ref), (zb_ref, wb_ref), (yc_ref, wc_ref))):
        gate = _sigmoid(_dot(h, wgate_ref[:, br * D_MODEL:(br + 1) * D_MODEL]))
        term = gate * _dot(in_ref[0], w_ref[...])
        merged = term if merged is None else merged + term
    y = _dot(merged.astype(BF16), wout_ref[...])
    o_ref[0] = x + gt1 * (_rms(y) * gpost_ref[...])


def _merge_call(x, mod, gpre, gpost, ya, zb, yc, wgate, wa, wb, wc, wout, layer):
    b, s, d = x.shape
    tm = ROW_TILE
    const2 = lambda bi, si: (0, 0)
    row3 = lambda bi, si: (bi, si, 0)
    return pl.pallas_call(
        _merge_kernel,
        grid=(b, s // tm),
        in_specs=[
            pl.BlockSpec((1, tm, d), row3),
            pl.BlockSpec((1, 1, N_MOD, d), lambda bi, si: (layer, bi, 0, 0)),
            pl.BlockSpec((1, d), const2),
            pl.BlockSpec((1, d), const2),
            pl.BlockSpec((1, tm, D_MIX), row3),
            pl.BlockSpec((1, tm, D_MIX), row3),
            pl.BlockSpec((1, tm, D_MIX), row3),
            pl.BlockSpec((d, N_BRANCH * d), const2),
            pl.BlockSpec((D_MIX, d), const2),
            pl.BlockSpec((D_MIX, d), const2),
            pl.BlockSpec((D_MIX, d), const2),
            pl.BlockSpec((d, d), const2),
        ],
        out_specs=pl.BlockSpec((1, tm, d), row3),
        out_shape=jax.ShapeDtypeStruct((b, s, d), F32),
        compiler_params=pltpu.CompilerParams(
            dimension_semantics=("arbitrary", "arbitrary"), vmem_limit_bytes=VMEM_LIMIT),
        name="mixer_merge",
    )(x, mod, gpre, gpost, ya, zb, yc, wgate, wa, wb, wc, wout)


def _mlp_kernel(x_ref, mod_ref, gpre_ref, gpost_ref, w1_ref, w2_ref, o_ref):
    x = x_ref[0]
    mod = mod_ref[0, 0]
    sh2 = mod[3:4]
    sc2 = mod[4:5]
    gt2 = mod[5:6]
    h = (_rms(x) * (gpre_ref[...] * (1.0 + sc2)) + sh2).astype(BF16)
    y = None
    for c in range(D_FF // D_MODEL):
        cols = slice(c * D_MODEL, (c + 1) * D_MODEL)
        a = jnp.maximum(_dot(h, w1_ref[:, cols]), 0.0)
        t = _dot((a * a).astype(BF16), w2_ref[cols, :])
        y = t if y is None else y + t
    o_ref[0] = x + gt2 * (_rms(y) * gpost_ref[...])


def _mlp_call(x, mod, gpre, gpost, w1, w2, layer):
    b, s, d = x.shape
    tm = ROW_TILE
    const2 = lambda bi, si: (0, 0)
    row3 = lambda bi, si: (bi, si, 0)
    return pl.pallas_call(
        _mlp_kernel,
        grid=(b, s // tm),
        in_specs=[
            pl.BlockSpec((1, tm, d), row3),
            pl.BlockSpec((1, 1, N_MOD, d), lambda bi, si: (layer, bi, 0, 0)),
            pl.BlockSpec((1, d), const2),
            pl.BlockSpec((1, d), const2),
            pl.BlockSpec((d, D_FF), const2),
            pl.BlockSpec((D_FF, d), const2),
        ],
        out_specs=pl.BlockSpec((1, tm, d), row3),
        out_shape=jax.ShapeDtypeStruct((b, s, d), F32),
        compiler_params=pltpu.CompilerParams(
            dimension_semantics=("arbitrary", "arbitrary"), vmem_limit_bytes=VMEM_LIMIT),
        name="relu2_mlp",
    )(x, mod, gpre, gpost, w1, w2)


def kernel(x, c, ada_w, ada_b, mix_pre_g, mix_post_g, mlp_pre_g, mlp_post_g, w_in, gmlp_ln_g, gmlp_ln_b, gmlp_ws, gmlp_bs, w_a_out, conv_w, conv_b, conv_ln_g, conv_ln_b, w_b_out, fox_bf, w_c_out, w_out, mlp_w1, mlp_w2):
    n_layers = ada_w.shape[0]
    b, s, d = x.shape
    assert d == D_MODEL and s % ROW_TILE == 0 and s % ATTN_TILE == 0
    assert ROW_TILE % CUM_CHUNK == 0 and ROW_TILE % CHUNK == 0

    mod = _ada_call(c, ada_w, ada_b).reshape(n_layers, b, N_MOD, d)

    o_glu = 2 * D_MIX
    o_qkv = o_glu + 2 * D_MIX
    o_f = o_qkv + 3 * D_MIX
    o_gate = o_f + N_HEADS

    row = lambda a: a.reshape(1, -1)
    for l in range(n_layers):
        w = w_in[l]
        wuv = w[:, :o_glu].astype(BF16)
        wglu = w[:, o_glu:o_qkv].astype(BF16)
        wqkv = w[:, o_qkv:o_f].astype(BF16)
        wf = jnp.pad(w[:, o_f:o_gate], ((0, 0), (0, LANES - N_HEADS))).astype(BF16)
        wgate = w[:, o_gate:].astype(BF16)
        bf = jnp.pad(fox_bf[l], (0, LANES - N_HEADS)).reshape(1, LANES)
        wsp = gmlp_ws[l].reshape(N_PAIRS, 2, CHUNK, CHUNK).transpose(0, 2, 1, 3)
        wsp = wsp.reshape(N_PAIRS, CHUNK, 2 * CHUNK).astype(BF16)
        bsp = jnp.repeat(gmlp_bs[l].T, GROUP_DIM, axis=1)
        cw = jnp.pad(conv_w[l], ((0, CONV_HALO - CONV_WIDTH), (0, 0)))

        ya, zb, qkv, fcol, frow = _mixer_in_call(
            x, mod, row(mix_pre_g[l]), wuv, wglu, wqkv, wf,
            row(gmlp_ln_g[l]), row(gmlp_ln_b[l]), wsp, bsp, cw, row(conv_b[l]),
            row(conv_ln_g[l]), row(conv_ln_b[l]), bf, l)
        yc = _attn_call(qkv, fcol, frow)
        x = _merge_call(
            x, mod, row(mix_pre_g[l]), row(mix_post_g[l]), ya, zb, yc, wgate,
            w_a_out[l].astype(BF16), w_b_out[l].astype(BF16), w_c_out[l].astype(BF16),
            w_out[l].astype(BF16), l)
        x = _mlp_call(x, mod, row(mlp_pre_g[l]), row(mlp_post_g[l]),
                      mlp_w1[l].astype(BF16), mlp_w2[l].astype(BF16), l)
    return x
```

```python
import functools
import math

import jax
import jax.numpy as jnp
from jax import lax
from jax.experimental import pallas as pl
from jax.experimental.pallas import tpu as pltpu

D_MODEL = 1024
D_MIX = 512
N_GROUPS = 8
CHUNK = 128
GROUP_DIM = D_MIX // N_GROUPS
CONV_WIDTH = 31
N_HEADS = 8
HEAD_DIM = D_MIX // N_HEADS
N_BRANCH = 3
D_FF = 4 * D_MODEL
N_MOD = 6
NORM_EPS = 1e-6

LANES = 128
N_PAIRS = D_MIX // LANES
CONV_HALO = 32
CUM_CHUNK = 256
VMEM_LIMIT = 56 * 1024 * 1024

ROW_TILE = 512
ATTN_TILE = 512
STAGE_CHUNK = 512
LOG2E = math.log2(math.e)

F32 = jnp.float32
BF16 = jnp.bfloat16


def _dot(a, b):
    return jnp.dot(a, b, preferred_element_type=F32)


def _sigmoid(x):
    return 0.5 * jnp.tanh(0.5 * x) + 0.5


def _gelu_tanh(x):
    c = math.sqrt(2.0 / math.pi)
    return 0.5 * x * (1.0 + jnp.tanh(c * (x + 0.044715 * (x * x * x))))


def _rms(x):
    return x * lax.rsqrt(jnp.mean(x * x, axis=-1, keepdims=True) + NORM_EPS)


def _layer_norm(x, g, b):
    mu = jnp.mean(x, axis=-1, keepdims=True)
    xc = x - mu
    var = jnp.mean(xc * xc, axis=-1, keepdims=True)
    return xc * lax.rsqrt(var + NORM_EPS) * g + b


def _split3(x):
    hi = x.astype(BF16)
    r = x - hi.astype(F32)
    mid = r.astype(BF16)
    lo = (r - mid.astype(F32)).astype(BF16)
    return hi, mid, lo


def _ada_kernel(c_ref, w_ref, b_ref, o_ref):
    c = c_ref[...]
    c_act = (c * _sigmoid(c)).astype(BF16)
    o_ref[0] = _dot(c_act, w_ref[0].astype(BF16)) + b_ref[0]


def _ada_call(c, ada_w, ada_b):
    n_layers, d, n_out = ada_w.shape
    b = c.shape[0]
    tn = 2 * D_MODEL
    return pl.pallas_call(
        _ada_kernel,
        grid=(n_layers, n_out // tn),
        in_specs=[
            pl.BlockSpec((b, d), lambda l, j: (0, 0)),
            pl.BlockSpec((1, d, tn), lambda l, j: (l, 0, j)),
            pl.BlockSpec((1, 1, tn), lambda l, j: (l, 0, j)),
        ],
        out_specs=pl.BlockSpec((1, b, tn), lambda l, j: (l, 0, j)),
        out_shape=jax.ShapeDtypeStruct((n_layers, b, n_out), F32),
        compiler_params=pltpu.CompilerParams(
            dimension_semantics=("arbitrary", "arbitrary"), vmem_limit_bytes=VMEM_LIMIT),
        name="ada_mod",
    )(c, ada_w, ada_b.reshape(n_layers, 1, n_out))


def _mixer_in_kernel(x_ref, mod_ref, g_ref, wuv_ref, wglu_ref, wqkv_ref, wf_ref,
                     lng_ref, lnb_ref, wsp_ref, bsp_ref, cw_ref, cb_ref, clg_ref, clb_ref,
                     bf_ref,
                     ya_ref, zb_ref, qkv_ref, frow_ref,
                     zs_ref, carry_ref):
    tm = x_ref.shape[1]
    s_idx = pl.program_id(1)

    @pl.when(s_idx == 0)
    def _():
        zs_ref[0:CONV_HALO, :] = jnp.zeros((CONV_HALO, D_MIX), F32)
        carry_ref[...] = jnp.zeros_like(carry_ref)

    x = x_ref[0]
    mod = mod_ref[0, 0]
    sh1 = mod[0:1]
    sc1 = mod[1:2]
    h = (_rms(x) * (g_ref[...] * (1.0 + sc1)) + sh1).astype(BF16)

    uv = _gelu_tanh(_dot(h, wuv_ref[...]))
    u_a = uv[:, :D_MIX]
    v_a = _layer_norm(uv[:, D_MIX:], lng_ref[...], lnb_ref[...]).astype(BF16)
    n_chunks = tm // CHUNK
    lane = lax.broadcasted_iota(jnp.int32, (CHUNK, LANES), 1)
    w_row = lax.broadcasted_iota(jnp.int32, (CHUNK, 2 * CHUNK), 0)
    w_col = lax.broadcasted_iota(jnp.int32, (CHUNK, 2 * CHUNK), 1)
    causal_w = (w_col % CHUNK) <= w_row
    zero_bf = jnp.zeros((CHUNK, LANES), BF16)
    for p in range(N_PAIRS):
        cols = slice(p * LANES, (p + 1) * LANES)
        w_pair = jnp.where(causal_w, wsp_ref[p], jnp.zeros_like(wsp_ref[p]))
        rhs = []
        for c in range(n_chunks):
            vc = v_a[c * CHUNK:(c + 1) * CHUNK, cols]
            rhs.append(jnp.concatenate(
                [jnp.where(lane < GROUP_DIM, vc, zero_bf),
                 jnp.where(lane >= GROUP_DIM, vc, zero_bf)], axis=0))
        sv = _dot(w_pair, jnp.concatenate(rhs, axis=1))
        bias = bsp_ref[:, cols]
        for c in range(n_chunks):
            rows = slice(c * CHUNK, (c + 1) * CHUNK)
            ya_ref[0, rows, cols] = (
                u_a[rows, cols] * (sv[:, c * LANES:(c + 1) * LANES] + bias)).astype(BF16)

    glu = _dot(h, wglu_ref[...])
    zs_ref[CONV_HALO:CONV_HALO + tm, :] = glu[:, :D_MIX] * _sigmoid(glu[:, D_MIX:])
    first = CONV_HALO - (CONV_WIDTH - 1)
    acc = jnp.broadcast_to(cb_ref[...], (tm, D_MIX))
    for j in range(CONV_WIDTH):
        acc = acc + cw_ref[j:j + 1, :] * zs_ref[first + j:first + j + tm, :]
    zs_ref[0:CONV_HALO, :] = zs_ref[tm:tm + CONV_HALO, :]
    zn = _layer_norm(acc, clg_ref[...], clb_ref[...])
    zb_ref[0] = (zn * _sigmoid(zn)).astype(BF16)

    qkv = _dot(h, wqkv_ref[...])
    scale = LOG2E / math.sqrt(HEAD_DIM)
    qkv_ref[0, :, 0:D_MIX] = (qkv[:, 0:D_MIX] * scale).astype(BF16)
    qkv_ref[0, :, D_MIX:] = qkv[:, D_MIX:].astype(BF16)

    f_raw = _dot(h, wf_ref[...]) + bf_ref[...]
    log_f = jnp.minimum(f_raw, 0.0) - jnp.log1p(jnp.exp(-jnp.abs(f_raw)))
    t_row = lax.broadcasted_iota(jnp.int32, (CUM_CHUNK, CUM_CHUNK), 0)
    t_col = lax.broadcasted_iota(jnp.int32, (CUM_CHUNK, CUM_CHUNK), 1)
    tri = jnp.where(t_col <= t_row, 1.0, 0.0).astype(BF16)
    carry = carry_ref[...]
    for c in range(tm // CUM_CHUNK):
        rows = slice(c * CUM_CHUNK, (c + 1) * CUM_CHUNK)
        hi, mid, lo = _split3(log_f[rows])
        cs = _dot(tri, hi) + _dot(tri, mid) + _dot(tri, lo) + carry
        carry = cs[CUM_CHUNK - 1:CUM_CHUNK, :]
        frow_ref[0, :, rows] = cs.T[0:N_HEADS, :]
    carry_ref[...] = carry


def _mixer_in_call(x, mod, g, wuv, wglu, wqkv, wf, lng, lnb, wsp, bsp, cw, cb, clg, clb, bf, layer):
    b, s, d = x.shape
    tm = ROW_TILE
    const2 = lambda bi, si: (0, 0)
    const3 = lambda bi, si: (0, 0, 0)
    row3 = lambda bi, si: (bi, si, 0)
    out_shapes = (
        jax.ShapeDtypeStruct((b, s, D_MIX), BF16),
        jax.ShapeDtypeStruct((b, s, D_MIX), BF16),
        jax.ShapeDtypeStruct((b, s, 3 * D_MIX), BF16),
        jax.ShapeDtypeStruct((b, N_HEADS, s), F32),
    )
    return pl.pallas_call(
        _mixer_in_kernel,
        grid=(b, s // tm),
        in_specs=[
            pl.BlockSpec((1, tm, d), row3),
            pl.BlockSpec((1, 1, N_MOD, d), lambda bi, si: (layer, bi, 0, 0)),
            pl.BlockSpec((1, d), const2),
            pl.BlockSpec((d, 2 * D_MIX), const2),
            pl.BlockSpec((d, 2 * D_MIX), const2),
            pl.BlockSpec((d, 3 * D_MIX), const2),
            pl.BlockSpec((d, LANES), const2),
            pl.BlockSpec((1, D_MIX), const2),
            pl.BlockSpec((1, D_MIX), const2),
            pl.BlockSpec((N_PAIRS, CHUNK, 2 * CHUNK), const3),
            pl.BlockSpec((CHUNK, D_MIX), const2),
            pl.BlockSpec((CONV_HALO, D_MIX), const2),
            pl.BlockSpec((1, D_MIX), const2),
            pl.BlockSpec((1, D_MIX), const2),
            pl.BlockSpec((1, D_MIX), const2),
            pl.BlockSpec((1, LANES), const2),
        ],
        out_specs=(
            pl.BlockSpec((1, tm, D_MIX), row3),
            pl.BlockSpec((1, tm, D_MIX), row3),
            pl.BlockSpec((1, tm, 3 * D_MIX), row3),
            pl.BlockSpec((1, N_HEADS, tm), lambda bi, si: (bi, 0, si)),
        ),
        out_shape=out_shapes,
        scratch_shapes=[
            pltpu.VMEM((CONV_HALO + tm, D_MIX), F32),
            pltpu.VMEM((1, LANES), F32),
        ],
        compiler_params=pltpu.CompilerParams(
            dimension_semantics=("arbitrary", "arbitrary"), vmem_limit_bytes=VMEM_LIMIT),
        name="mixer_in",
    )(x, mod, g, wuv, wglu, wqkv, wf, lng, lnb, wsp, bsp, cw, cb, clg, clb, bf)


def _aug_key_rows(f):
    n = f.shape[1]
    hi, mid, lo = _split3(f)
    sub = lax.broadcasted_iota(jnp.int32, (N_HEADS, n), 0)
    ones3 = jnp.where(sub < 3, 1.0, 0.0)
    return jnp.concatenate(
        [-hi.astype(F32), -mid.astype(F32), -lo.astype(F32), jnp.zeros((N_HEADS, n), F32),
         ones3, jnp.zeros((LANES - 5 * N_HEADS, n), F32)], axis=0)


def _aug_query_rows(f, head, row):
    hi, mid, lo = _split3(f)
    ones = (row == head) | (row == head + N_HEADS) | (row == head + 2 * N_HEADS)
    base = jnp.where(ones, 1.0, 0.0)
    return jnp.where(row == 4 * N_HEADS, hi.astype(F32),
                     jnp.where(row == 4 * N_HEADS + 1, mid.astype(F32),
                               jnp.where(row == 4 * N_HEADS + 2, lo.astype(F32), base)))


def _attn_kernel(q_ref, k_ref, v_ref, frow_ref, o_ref,
                 kk_ref, vta_ref, vtb_ref, m_ref, acca_ref, accb_ref):
    s_len = k_ref.shape[1]
    tq = q_ref.shape[1]
    tk = tq
    pair = pl.program_id(1)
    qi = pl.program_id(2)

    @pl.when(qi == 0)
    def _stage_keys_values():
        rowc = lax.broadcasted_iota(jnp.int32, (LANES, STAGE_CHUNK), 0)
        for c in range(s_len // STAGE_CHUNK):
            cols = slice(c * STAGE_CHUNK, (c + 1) * STAGE_CHUNK)
            kk_ref[cols, 0:LANES] = k_ref[0, cols, :]
            aug_t = _aug_key_rows(frow_ref[0, :, cols] * LOG2E)
            kk_ref[cols, LANES:2 * LANES] = aug_t.T.astype(BF16)
            vt = v_ref[0, cols, :].astype(F32).T
            vta_ref[:, cols] = jnp.where(rowc < HEAD_DIM, vt, 1.0).astype(BF16)
            vtb_ref[:, cols] = jnp.where(rowc < HEAD_DIM, 1.0, vt).astype(BF16)

    row = lax.broadcasted_iota(jnp.int32, (LANES, tq), 0)
    lo_rows = row < HEAD_DIM
    q_cols = pl.ds(pl.multiple_of(qi * tq, tq), tq)
    qt = q_ref[0].astype(F32).T
    top = jnp.concatenate([jnp.where(lo_rows, qt, 0.0), jnp.where(lo_rows, 0.0, qt)], axis=1)
    aug = []
    for hh in range(2):
        head = 2 * pair + hh
        f_q = frow_ref[0, pl.ds(head, 1), q_cols] * LOG2E
        aug.append(_aug_query_rows(f_q, head, row))
    qq = jnp.concatenate([top, jnp.concatenate(aug, axis=1)], axis=0).astype(BF16)

    m_ref[...] = jnp.full(m_ref.shape, -1e30, F32)
    acca_ref[...] = jnp.zeros_like(acca_ref)
    accb_ref[...] = jnp.zeros_like(accb_ref)

    def step(j, masked):
        start = pl.multiple_of(j * tk, tk)
        st = _dot(kk_ref[pl.ds(start, tk), :], qq)
        if masked:
            r = lax.broadcasted_iota(jnp.int32, (tk, 2 * tq), 0)
            c = lax.broadcasted_iota(jnp.int32, (tk, 2 * tq), 1)
            st = jnp.where(r <= jnp.where(c >= tq, c - tq, c), st, -jnp.inf)
        m_old = m_ref[...]
        m_new = jnp.maximum(m_old, jnp.max(st, axis=0, keepdims=True))
        alpha = jnp.exp2(m_old - m_new)
        m_ref[...] = m_new
        pt = jnp.exp2(st - m_new).astype(BF16)
        acca_ref[...] = (alpha[:, :tq] * acca_ref[...]
                         + _dot(vta_ref[:, pl.ds(start, tk)], pt[:, :tq]))
        accb_ref[...] = (alpha[:, tq:] * accb_ref[...]
                         + _dot(vtb_ref[:, pl.ds(start, tk)], pt[:, tq:]))

    def body(j, carry):
        step(j, masked=False)
        return carry

    lax.fori_loop(0, qi, body, 0)
    step(qi, masked=True)

    inv_a = 1.0 / acca_ref[HEAD_DIM:HEAD_DIM + 1, :]
    inv_b = 1.0 / accb_ref[0:1, :]
    ot = jnp.where(lo_rows, acca_ref[...] * inv_a, accb_ref[...] * inv_b)
    o_ref[0] = ot.T.astype(BF16)


def _attn_call(qkv, frow):
    b, s, _ = qkv.shape
    tq = ATTN_TILE
    return pl.pallas_call(
        _attn_kernel,
        grid=(b, N_PAIRS, s // tq),
        in_specs=[
            pl.BlockSpec((1, tq, LANES), lambda bi, p, i: (bi, i, p)),
            pl.BlockSpec((1, s, LANES), lambda bi, p, i: (bi, 0, N_PAIRS + p)),
            pl.BlockSpec((1, s, LANES), lambda bi, p, i: (bi, 0, 2 * N_PAIRS + p)),
            pl.BlockSpec((1, N_HEADS, s), lambda bi, p, i: (bi, 0, 0)),
        ],
        out_specs=pl.BlockSpec((1, tq, LANES), lambda bi, p, i: (bi, i, p)),
        out_shape=jax.ShapeDtypeStruct((b, s, D_MIX), BF16),
        scratch_shapes=[
            pltpu.VMEM((s, 2 * LANES), BF16),
            pltpu.VMEM((LANES, s), BF16),
            pltpu.VMEM((LANES, s), BF16),
            pltpu.VMEM((1, 2 * tq), F32),
            pltpu.VMEM((LANES, tq), F32),
            pltpu.VMEM((LANES, tq), F32),
        ],
        compiler_params=pltpu.CompilerParams(
            dimension_semantics=("arbitrary", "arbitrary", "arbitrary"),
            vmem_limit_bytes=VMEM_LIMIT),
        name="fox_attn",
    )(qkv, qkv, qkv, frow)


def _merge_kernel(x_ref, mod_ref, gpre_ref, gpost_ref, ya_ref, zb_ref, yc_ref,
                  wgate_ref, wa_ref, wb_ref, wc_ref, wout_ref, o_ref):
    x = x_ref[0]
    mod = mod_ref[0, 0]
    sh1 = mod[0:1]
    sc1 = mod[1:2]
    gt1 = mod[2:3]
    h = (_rms(x) * (gpre_ref[...] * (1.0 + sc1)) + sh1).astype(BF16)
    merged = None
    for br, (in_ref, w_ref) in enumerate(((ya_ref, wa_ref), (zb_ref, wb_ref), (yc_ref, wc_ref))):
        gate = _sigmoid(_dot(h, wgate_ref[:, br * D_MODEL:(br + 1) * D_MODEL]))
        term = gate * _dot(in_ref[0], w_ref[...])
        merged = term if merged is None else merged + term
    y = _dot(merged.astype(BF16), wout_ref[...])
    o_ref[0] = x + gt1 * (_rms(y) * gpost_ref[...])


def _merge_call(x, mod, gpre, gpost, ya, zb, yc, wgate, wa, wb, wc, wout, layer):
    b, s, d = x.shape
    tm = ROW_TILE
    const2 = lambda bi, si: (0, 0)
    row3 = lambda bi, si: (bi, si, 0)
    return pl.pallas_call(
        _merge_kernel,
        grid=(b, s // tm),
        in_specs=[
            pl.BlockSpec((1, tm, d), row3),
            pl.BlockSpec((1, 1, N_MOD, d), lambda bi, si: (layer, bi, 0, 0)),
            pl.BlockSpec((1, d), const2),
            pl.BlockSpec((1, d), const2),
            pl.BlockSpec((1, tm, D_MIX), row3),
            pl.BlockSpec((1, tm, D_MIX), row3),
            pl.BlockSpec((1, tm, D_MIX), row3),
            pl.BlockSpec((d, N_BRANCH * d), const2),
            pl.BlockSpec((D_MIX, d), const2),
            pl.BlockSpec((D_MIX, d), const2),
            pl.BlockSpec((D_MIX, d), const2),
            pl.BlockSpec((d, d), const2),
        ],
        out_specs=pl.BlockSpec((1, tm, d), row3),
        out_shape=jax.ShapeDtypeStruct((b, s, d), F32),
        compiler_params=pltpu.CompilerParams(
            dimension_semantics=("arbitrary", "arbitrary"), vmem_limit_bytes=VMEM_LIMIT),
        name="mixer_merge",
    )(x, mod, gpre, gpost, ya, zb, yc, wgate, wa, wb, wc, wout)


def _mlp_kernel(x_ref, mod_ref, gpre_ref, gpost_ref, w1_ref, w2_ref, o_ref):
    x = x_ref[0]
    mod = mod_ref[0, 0]
    sh2 = mod[3:4]
    sc2 = mod[4:5]
    gt2 = mod[5:6]
    h = (_rms(x) * (gpre_ref[...] * (1.0 + sc2)) + sh2).astype(BF16)
    y = None
    for c in range(D_FF // D_MODEL):
        cols = slice(c * D_MODEL, (c + 1) * D_MODEL)
        a = jnp.maximum(_dot(h, w1_ref[:, cols]), 0.0)
        t = _dot((a * a).astype(BF16), w2_ref[cols, :])
        y = t if y is None else y + t
    o_ref[0] = x + gt2 * (_rms(y) * gpost_ref[...])


def _mlp_call(x, mod, gpre, gpost, w1, w2, layer):
    b, s, d = x.shape
    tm = ROW_TILE
    const2 = lambda bi, si: (0, 0)
    row3 = lambda bi, si: (bi, si, 0)
    return pl.pallas_call(
        _mlp_kernel,
        grid=(b, s // tm),
        in_specs=[
            pl.BlockSpec((1, tm, d), row3),
            pl.BlockSpec((1, 1, N_MOD, d), lambda bi, si: (layer, bi, 0, 0)),
            pl.BlockSpec((1, d), const2),
            pl.BlockSpec((1, d), const2),
            pl.BlockSpec((d, D_FF), const2),
            pl.BlockSpec((D_FF, d), const2),
        ],
        out_specs=pl.BlockSpec((1, tm, d), row3),
        out_shape=jax.ShapeDtypeStruct((b, s, d), F32),
        compiler_params=pltpu.CompilerParams(
            dimension_semantics=("arbitrary", "arbitrary"), vmem_limit_bytes=VMEM_LIMIT),
        name="relu2_mlp",
    )(x, mod, gpre, gpost, w1, w2)


def kernel(x, c, ada_w, ada_b, mix_pre_g, mix_post_g, mlp_pre_g, mlp_post_g, w_in, gmlp_ln_g, gmlp_ln_b, gmlp_ws, gmlp_bs, w_a_out, conv_w, conv_b, conv_ln_g, conv_ln_b, w_b_out, fox_bf, w_c_out, w_out, mlp_w1, mlp_w2):
    n_layers = ada_w.shape[0]
    b, s, d = x.shape
    assert d == D_MODEL and s % ROW_TILE == 0 and s % ATTN_TILE == 0
    assert ROW_TILE % CUM_CHUNK == 0 and ROW_TILE % CHUNK == 0

    mod = _ada_call(c, ada_w, ada_b).reshape(n_layers, b, N_MOD, d)

    o_glu = 2 * D_MIX
    o_qkv = o_glu + 2 * D_MIX
    o_f = o_qkv + 3 * D_MIX
    o_gate = o_f + N_HEADS

    row = lambda a: a.reshape(1, -1)
    for l in range(n_layers):
        w = w_in[l]
        wuv = w[:, :o_glu].astype(BF16)
        wglu = w[:, o_glu:o_qkv].astype(BF16)
        wqkv = w[:, o_qkv:o_f].astype(BF16)
        wf = jnp.pad(w[:, o_f:o_gate], ((0, 0), (0, LANES - N_HEADS))).astype(BF16)
        wgate = w[:, o_gate:].astype(BF16)
        bf = jnp.pad(fox_bf[l], (0, LANES - N_HEADS)).reshape(1, LANES)
        wsp = gmlp_ws[l].reshape(N_PAIRS, 2, CHUNK, CHUNK).transpose(0, 2, 1, 3)
        wsp = wsp.reshape(N_PAIRS, CHUNK, 2 * CHUNK).astype(BF16)
        bsp = jnp.repeat(gmlp_bs[l].T, GROUP_DIM, axis=1)
        cw = jnp.pad(conv_w[l], ((0, CONV_HALO - CONV_WIDTH), (0, 0)))

        ya, zb, qkv, frow = _mixer_in_call(
            x, mod, row(mix_pre_g[l]), wuv, wglu, wqkv, wf,
            row(gmlp_ln_g[l]), row(gmlp_ln_b[l]), wsp, bsp, cw, row(conv_b[l]),
            row(conv_ln_g[l]), row(conv_ln_b[l]), bf, l)
        yc = _attn_call(qkv, frow)
        x = _merge_call(
            x, mod, row(mix_pre_g[l]), row(mix_post_g[l]), ya, zb, yc, wgate,
            w_a_out[l].astype(BF16), w_b_out[l].astype(BF16), w_c_out[l].astype(BF16),
            w_out[l].astype(BF16), l)
        x = _mlp_call(x, mod, row(mlp_pre_g[l]), row(mlp_post_g[l]),
                      mlp_w1[l].astype(BF16), mlp_w2[l].astype(BF16), l)
    return x
```

```python
import functools
import math

import jax
import jax.numpy as jnp
from jax import lax
from jax.experimental import pallas as pl
from jax.experimental.pallas import tpu as pltpu

D_MODEL = 1024
D_MIX = 512
N_GROUPS = 8
CHUNK = 128
GROUP_DIM = D_MIX // N_GROUPS
CONV_WIDTH = 31
N_HEADS = 8
HEAD_DIM = D_MIX // N_HEADS
N_BRANCH = 3
D_FF = 4 * D_MODEL
N_MOD = 6
NORM_EPS = 1e-6

LANES = 128
SUBLANES = 8
N_PAIRS = D_MIX // LANES
CONV_HALO = 32
CUM_CHUNK = 256
VMEM_LIMIT = 56 * 1024 * 1024

ROW_TILE = 512
ATTN_TILE = 512
STAGE_CHUNK = 512
LOG2E = math.log2(math.e)

F32 = jnp.float32
BF16 = jnp.bfloat16


def _dot(a, b):
    return jnp.dot(a, b, preferred_element_type=F32)


def _sigmoid(x):
    return 0.5 * jnp.tanh(0.5 * x) + 0.5


def _gelu_tanh(x):
    c = math.sqrt(2.0 / math.pi)
    return 0.5 * x * (1.0 + jnp.tanh(c * (x + 0.044715 * (x * x * x))))


def _rms(x):
    return x * lax.rsqrt(jnp.mean(x * x, axis=-1, keepdims=True) + NORM_EPS)


def _layer_norm(x, g, b):
    mu = jnp.mean(x, axis=-1, keepdims=True)
    xc = x - mu
    var = jnp.mean(xc * xc, axis=-1, keepdims=True)
    return xc * lax.rsqrt(var + NORM_EPS) * g + b


def _split3(x):
    hi = x.astype(BF16)
    r = x - hi.astype(F32)
    mid = r.astype(BF16)
    lo = (r - mid.astype(F32)).astype(BF16)
    return hi, mid, lo


def _ada_kernel(c_ref, w_ref, b_ref, o_ref):
    c = c_ref[...]
    c_act = (c * _sigmoid(c)).astype(BF16)
    o_ref[0] = _dot(c_act, w_ref[0].astype(BF16)) + b_ref[0]


def _ada_call(c, ada_w, ada_b):
    n_layers, d, n_out = ada_w.shape
    b = c.shape[0]
    tn = 2 * D_MODEL
    return pl.pallas_call(
        _ada_kernel,
        grid=(n_layers, n_out // tn),
        in_specs=[
            pl.BlockSpec((b, d), lambda l, j: (0, 0)),
            pl.BlockSpec((1, d, tn), lambda l, j: (l, 0, j)),
            pl.BlockSpec((1, 1, tn), lambda l, j: (l, 0, j)),
        ],
        out_specs=pl.BlockSpec((1, b, tn), lambda l, j: (l, 0, j)),
        out_shape=jax.ShapeDtypeStruct((n_layers, b, n_out), F32),
        compiler_params=pltpu.CompilerParams(
            dimension_semantics=("arbitrary", "arbitrary"), vmem_limit_bytes=VMEM_LIMIT),
        name="ada_mod",
    )(c, ada_w, ada_b.reshape(n_layers, 1, n_out))


def _mixer_in_kernel(x_ref, mod_ref, g_ref, wuv_ref, wglu_ref, wqkv_ref, wf_ref,
                     lng_ref, lnb_ref, wsp_ref, bsp_ref, cw_ref, cb_ref, clg_ref, clb_ref,
                     bf_ref,
                     ya_ref, zb_ref, qkv_ref, frow_ref,
                     zs_ref, sh_ref, carry_ref):
    tm = x_ref.shape[1]
    s_idx = pl.program_id(1)

    @pl.when(s_idx == 0)
    def _():
        zs_ref[0:CONV_HALO, :] = jnp.zeros((CONV_HALO, D_MIX), F32)
        carry_ref[...] = jnp.zeros_like(carry_ref)

    x = x_ref[0]
    mod = mod_ref[0, 0]
    sh1 = mod[0:1]
    sc1 = mod[1:2]
    h = (_rms(x) * (g_ref[...] * (1.0 + sc1)) + sh1).astype(BF16)

    uv = _gelu_tanh(_dot(h, wuv_ref[...]))
    u_a = uv[:, :D_MIX]
    v_a = _layer_norm(uv[:, D_MIX:], lng_ref[...], lnb_ref[...]).astype(BF16)
    n_chunks = tm // CHUNK
    lane = lax.broadcasted_iota(jnp.int32, (CHUNK, LANES), 1)
    w_row = lax.broadcasted_iota(jnp.int32, (CHUNK, 2 * CHUNK), 0)
    w_col = lax.broadcasted_iota(jnp.int32, (CHUNK, 2 * CHUNK), 1)
    causal_w = (w_col % CHUNK) <= w_row
    zero_bf = jnp.zeros((CHUNK, LANES), BF16)
    for p in range(N_PAIRS):
        cols = slice(p * LANES, (p + 1) * LANES)
        w_pair = jnp.where(causal_w, wsp_ref[p], jnp.zeros_like(wsp_ref[p]))
        rhs = []
        for c in range(n_chunks):
            vc = v_a[c * CHUNK:(c + 1) * CHUNK, cols]
            rhs.append(jnp.concatenate(
                [jnp.where(lane < GROUP_DIM, vc, zero_bf),
                 jnp.where(lane >= GROUP_DIM, vc, zero_bf)], axis=0))
        sv = _dot(w_pair, jnp.concatenate(rhs, axis=1))
        bias = bsp_ref[:, cols]
        for c in range(n_chunks):
            rows = slice(c * CHUNK, (c + 1) * CHUNK)
            ya_ref[0, rows, cols] = (
                u_a[rows, cols] * (sv[:, c * LANES:(c + 1) * LANES] + bias)).astype(BF16)

    glu = _dot(h, wglu_ref[...])
    zs_ref[CONV_HALO:CONV_HALO + tm, :] = glu[:, :D_MIX] * _sigmoid(glu[:, D_MIX:])
    first = CONV_HALO - (CONV_WIDTH - 1)
    n_rows = tm + CONV_HALO - SUBLANES
    conv = []
    for c in range(N_PAIRS):
        cols = slice(c * LANES, (c + 1) * LANES)
        for r in range(1, SUBLANES):
            sh_ref[c, r - 1] = zs_ref[r:r + n_rows, cols]
        acc = jnp.broadcast_to(cb_ref[:, cols], (tm, LANES))
        for j in range(CONV_WIDTH):
            r = (first + j) % SUBLANES
            base = first + j - r
            src = (zs_ref[base:base + tm, cols] if r == 0
                   else sh_ref[c, r - 1, base:base + tm, :])
            acc = acc + cw_ref[j:j + 1, cols] * src
        conv.append(acc)
    zs_ref[0:CONV_HALO, :] = zs_ref[tm:tm + CONV_HALO, :]
    zn = _layer_norm(jnp.concatenate(conv, axis=1), clg_ref[...], clb_ref[...])
    zb_ref[0] = (zn * _sigmoid(zn)).astype(BF16)

    qkv = _dot(h, wqkv_ref[...])
    scale = LOG2E / math.sqrt(HEAD_DIM)
    qkv_ref[0, :, 0:D_MIX] = (qkv[:, 0:D_MIX] * scale).astype(BF16)
    qkv_ref[0, :, D_MIX:] = qkv[:, D_MIX:].astype(BF16)

    f_raw = _dot(h, wf_ref[...]) + bf_ref[...]
    log_f = jnp.minimum(f_raw, 0.0) - jnp.log1p(jnp.exp(-jnp.abs(f_raw)))
    t_row = lax.broadcasted_iota(jnp.int32, (CUM_CHUNK, CUM_CHUNK), 0)
    t_col = lax.broadcasted_iota(jnp.int32, (CUM_CHUNK, CUM_CHUNK), 1)
    tri = jnp.where(t_col <= t_row, 1.0, 0.0).astype(BF16)
    carry = carry_ref[...]
    for c in range(tm // CUM_CHUNK):
        rows = slice(c * CUM_CHUNK, (c + 1) * CUM_CHUNK)
        hi, mid, lo = _split3(log_f[rows])
        cs = _dot(tri, hi) + _dot(tri, mid) + _dot(tri, lo) + carry
        carry = cs[CUM_CHUNK - 1:CUM_CHUNK, :]
        frow_ref[0, :, rows] = cs.T[0:N_HEADS, :]
    carry_ref[...] = carry


def _mixer_in_call(x, mod, g, wuv, wglu, wqkv, wf, lng, lnb, wsp, bsp, cw, cb, clg, clb, bf, layer):
    b, s, d = x.shape
    tm = ROW_TILE
    const2 = lambda bi, si: (0, 0)
    const3 = lambda bi, si: (0, 0, 0)
    row3 = lambda bi, si: (bi, si, 0)
    out_shapes = (
        jax.ShapeDtypeStruct((b, s, D_MIX), BF16),
        jax.ShapeDtypeStruct((b, s, D_MIX), BF16),
        jax.ShapeDtypeStruct((b, s, 3 * D_MIX), BF16),
        jax.ShapeDtypeStruct((b, N_HEADS, s), F32),
    )
    return pl.pallas_call(
        _mixer_in_kernel,
        grid=(b, s // tm),
        in_specs=[
            pl.BlockSpec((1, tm, d), row3),
            pl.BlockSpec((1, 1, N_MOD, d), lambda bi, si: (layer, bi, 0, 0)),
            pl.BlockSpec((1, d), const2),
            pl.BlockSpec((d, 2 * D_MIX), const2),
            pl.BlockSpec((d, 2 * D_MIX), const2),
            pl.BlockSpec((d, 3 * D_MIX), const2),
            pl.BlockSpec((d, LANES), const2),
            pl.BlockSpec((1, D_MIX), const2),
            pl.BlockSpec((1, D_MIX), const2),
            pl.BlockSpec((N_PAIRS, CHUNK, 2 * CHUNK), const3),
            pl.BlockSpec((CHUNK, D_MIX), const2),
            pl.BlockSpec((CONV_HALO, D_MIX), const2),
            pl.BlockSpec((1, D_MIX), const2),
            pl.BlockSpec((1, D_MIX), const2),
            pl.BlockSpec((1, D_MIX), const2),
            pl.BlockSpec((1, LANES), const2),
        ],
        out_specs=(
            pl.BlockSpec((1, tm, D_MIX), row3),
            pl.BlockSpec((1, tm, D_MIX), row3),
            pl.BlockSpec((1, tm, 3 * D_MIX), row3),
            pl.BlockSpec((1, N_HEADS, tm), lambda bi, si: (bi, 0, si)),
        ),
        out_shape=out_shapes,
        scratch_shapes=[
            pltpu.VMEM((CONV_HALO + tm, D_MIX), F32),
            pltpu.VMEM((N_PAIRS, SUBLANES - 1, tm + CONV_HALO - SUBLANES, LANES), F32),
            pltpu.VMEM((1, LANES), F32),
        ],
        compiler_params=pltpu.CompilerParams(
            dimension_semantics=("arbitrary", "arbitrary"), vmem_limit_bytes=VMEM_LIMIT),
        name="mixer_in",
    )(x, mod, g, wuv, wglu, wqkv, wf, lng, lnb, wsp, bsp, cw, cb, clg, clb, bf)


def _aug_key_rows(f):
    n = f.shape[1]
    hi, mid, lo = _split3(f)
    sub = lax.broadcasted_iota(jnp.int32, (N_HEADS, n), 0)
    ones3 = jnp.where(sub < 3, 1.0, 0.0)
    return jnp.concatenate(
        [-hi.astype(F32), -mid.astype(F32), -lo.astype(F32), jnp.zeros((N_HEADS, n), F32),
         ones3, jnp.zeros((LANES - 5 * N_HEADS, n), F32)], axis=0)


def _aug_query_rows(f, head, row):
    hi, mid, lo = _split3(f)
    ones = (row == head) | (row == head + N_HEADS) | (row == head + 2 * N_HEADS)
    base = jnp.where(ones, 1.0, 0.0)
    return jnp.where(row == 4 * N_HEADS, hi.astype(F32),
                     jnp.where(row == 4 * N_HEADS + 1, mid.astype(F32),
                               jnp.where(row == 4 * N_HEADS + 2, lo.astype(F32), base)))


def _attn_kernel(q_ref, k_ref, v_ref, frow_ref, o_ref,
                 kk_ref, vta_ref, vtb_ref, qq_ref, sta_ref, stb_ref, mba_ref, mbb_ref,
                 m_ref, acca_ref, accb_ref):
    s_len = k_ref.shape[1]
    tq = q_ref.shape[1]
    tk = tq
    pair = pl.program_id(1)
    qi = pl.program_id(2)

    @pl.when(qi == 0)
    def _stage_keys_values():
        rowc = lax.broadcasted_iota(jnp.int32, (LANES, STAGE_CHUNK), 0)
        for c in range(s_len // STAGE_CHUNK):
            cols = slice(c * STAGE_CHUNK, (c + 1) * STAGE_CHUNK)
            kk_ref[cols, 0:LANES] = k_ref[0, cols, :]
            aug_t = _aug_key_rows(frow_ref[0, :, cols] * LOG2E)
            kk_ref[cols, LANES:2 * LANES] = aug_t.T.astype(BF16)
            vt = v_ref[0, cols, :].astype(F32).T
            vta_ref[:, cols] = jnp.where(rowc < HEAD_DIM, vt, 1.0).astype(BF16)
            vtb_ref[:, cols] = jnp.where(rowc < HEAD_DIM, 1.0, vt).astype(BF16)

    row = lax.broadcasted_iota(jnp.int32, (LANES, tq), 0)
    lo_rows = row < HEAD_DIM
    q_cols = pl.ds(pl.multiple_of(qi * tq, tq), tq)
    qt = q_ref[0].astype(F32).T
    top = jnp.concatenate([jnp.where(lo_rows, qt, 0.0), jnp.where(lo_rows, 0.0, qt)], axis=1)
    aug = []
    for hh in range(2):
        head = 2 * pair + hh
        f_q = frow_ref[0, pl.ds(head, 1), q_cols] * LOG2E
        aug.append(_aug_query_rows(f_q, head, row))
    qq_ref[...] = jnp.concatenate(
        [top, jnp.concatenate(aug, axis=1)], axis=0).astype(BF16)

    m_ref[...] = jnp.full(m_ref.shape, -1e30, F32)
    acca_ref[...] = jnp.zeros_like(acca_ref)
    accb_ref[...] = jnp.zeros_like(accb_ref)

    def produce(j, buf, masked):
        st_ref, mb_ref = buf
        start = pl.multiple_of(j * tk, tk)
        st = _dot(kk_ref[pl.ds(start, tk), :], qq_ref[...])
        if masked:
            r = lax.broadcasted_iota(jnp.int32, (tk, 2 * tq), 0)
            c = lax.broadcasted_iota(jnp.int32, (tk, 2 * tq), 1)
            st = jnp.where(r <= jnp.where(c >= tq, c - tq, c), st, -jnp.inf)
        st_ref[...] = st
        mb_ref[...] = jnp.max(st, axis=0, keepdims=True)

    def consume(j, buf):
        st_ref, mb_ref = buf
        start = pl.multiple_of(j * tk, tk)
        m_old = m_ref[...]
        m_new = jnp.maximum(m_old, mb_ref[...])
        alpha = jnp.exp2(m_old - m_new)
        m_ref[...] = m_new
        pt = jnp.exp2(st_ref[...] - m_new).astype(BF16)
        acca_ref[...] = (alpha[:, :tq] * acca_ref[...]
                         + _dot(vta_ref[:, pl.ds(start, tk)], pt[:, :tq]))
        accb_ref[...] = (alpha[:, tq:] * accb_ref[...]
                         + _dot(vtb_ref[:, pl.ds(start, tk)], pt[:, tq:]))

    buf_a = (sta_ref, mba_ref)
    buf_b = (stb_ref, mbb_ref)

    @pl.when(qi == 0)
    def _only_diagonal():
        produce(0, buf_a, masked=True)
        consume(0, buf_a)

    @pl.when(qi > 0)
    def _general():
        produce(0, buf_a, masked=False)
        n_loop = (qi - 1) // 2

        def body(i, carry):
            j = 2 * i
            produce(j + 1, buf_b, masked=False)
            consume(j, buf_a)
            produce(j + 2, buf_a, masked=False)
            consume(j + 1, buf_b)
            return carry

        lax.fori_loop(0, n_loop, body, 0)
        done = 2 * n_loop

        @pl.when(qi - done == 1)
        def _tail_two():
            produce(qi, buf_b, masked=True)
            consume(done, buf_a)
            consume(qi, buf_b)

        @pl.when(qi - done == 2)
        def _tail_three():
            produce(done + 1, buf_b, masked=False)
            consume(done, buf_a)
            produce(qi, buf_a, masked=True)
            consume(done + 1, buf_b)
            consume(qi, buf_a)

    inv_a = 1.0 / acca_ref[HEAD_DIM:HEAD_DIM + 1, :]
    inv_b = 1.0 / accb_ref[0:1, :]
    ot = jnp.where(lo_rows, acca_ref[...] * inv_a, accb_ref[...] * inv_b)
    o_ref[0] = ot.T.astype(BF16)


def _attn_call(qkv, frow):
    b, s, _ = qkv.shape
    tq = ATTN_TILE
    return pl.pallas_call(
        _attn_kernel,
        grid=(b, N_PAIRS, s // tq),
        in_specs=[
            pl.BlockSpec((1, tq, LANES), lambda bi, p, i: (bi, i, p)),
            pl.BlockSpec((1, s, LANES), lambda bi, p, i: (bi, 0, N_PAIRS + p)),
            pl.BlockSpec((1, s, LANES), lambda bi, p, i: (bi, 0, 2 * N_PAIRS + p)),
            pl.BlockSpec((1, N_HEADS, s), lambda bi, p, i: (bi, 0, 0)),
        ],
        out_specs=pl.BlockSpec((1, tq, LANES), lambda bi, p, i: (bi, i, p)),
        out_shape=jax.ShapeDtypeStruct((b, s, D_MIX), BF16),
        scratch_shapes=[
            pltpu.VMEM((s, 2 * LANES), BF16),
            pltpu.VMEM((LANES, s), BF16),
            pltpu.VMEM((LANES, s), BF16),
            pltpu.VMEM((2 * LANES, 2 * tq), BF16),
            pltpu.VMEM((tq, 2 * tq), F32),
            pltpu.VMEM((tq, 2 * tq), F32),
            pltpu.VMEM((1, 2 * tq), F32),
            pltpu.VMEM((1, 2 * tq), F32),
            pltpu.VMEM((1, 2 * tq), F32),
            pltpu.VMEM((LANES, tq), F32),
            pltpu.VMEM((LANES, tq), F32),
        ],
        compiler_params=pltpu.CompilerParams(
            dimension_semantics=("arbitrary", "arbitrary", "arbitrary"),
            vmem_limit_bytes=VMEM_LIMIT),
        name="fox_attn",
    )(qkv, qkv, qkv, frow)


def _merge_kernel(x_ref, mod_ref, gpre_ref, gpost_ref, ya_ref, zb_ref, yc_ref,
                  wgate_ref, wa_ref, wb_ref, wc_ref, wout_ref, o_ref):
    x = x_ref[0]
    mod = mod_ref[0, 0]
    sh1 = mod[0:1]
    sc1 = mod[1:2]
    gt1 = mod[2:3]
    h = (_rms(x) * (gpre_ref[...] * (1.0 + sc1)) + sh1).astype(BF16)
    merged = None
    for br, (in_ref, w_ref) in enumerate(((ya_ref, wa_ref), (zb_ref, wb_ref), (yc_ref, wc_ref))):
        gate = _sigmoid(_dot(h, wgate_ref[:, br * D_MODEL:(br + 1) * D_MODEL]))
        term = gate * _dot(in_ref[0], w_ref[...])
        merged = term if merged is None else merged + term
    y = _dot(merged.astype(BF16), wout_ref[...])
    o_ref[0] = x + gt1 * (_rms(y) * gpost_ref[...])


def _merge_call(x, mod, gpre, gpost, ya, zb, yc, wgate, wa, wb, wc, wout, layer):
    b, s, d = x.shape
    tm = ROW_TILE
    const2 = lambda bi, si: (0, 0)
    row3 = lambda bi, si: (bi, si, 0)
    return pl.pallas_call(
        _merge_kernel,
        grid=(b, s // tm),
        in_specs=[
            pl.BlockSpec((1, tm, d), row3),
            pl.BlockSpec((1, 1, N_MOD, d), lambda bi, si: (layer, bi, 0, 0)),
            pl.BlockSpec((1, d), const2),
            pl.BlockSpec((1, d), const2),
            pl.BlockSpec((1, tm, D_MIX), row3),
            pl.BlockSpec((1, tm, D_MIX), row3),
            pl.BlockSpec((1, tm, D_MIX), row3),
            pl.BlockSpec((d, N_BRANCH * d), const2),
            pl.BlockSpec((D_MIX, d), const2),
            pl.BlockSpec((D_MIX, d), const2),
            pl.BlockSpec((D_MIX, d), const2),
            pl.BlockSpec((d, d), const2),
        ],
        out_specs=pl.BlockSpec((1, tm, d), row3),
        out_shape=jax.ShapeDtypeStruct((b, s, d), F32),
        compiler_params=pltpu.CompilerParams(
            dimension_semantics=("arbitrary", "arbitrary"), vmem_limit_bytes=VMEM_LIMIT),
        name="mixer_merge",
    )(x, mod, gpre, gpost, ya, zb, yc, wgate, wa, wb, wc, wout)


def _mlp_kernel(x_ref, mod_ref, gpre_ref, gpost_ref, w1_ref, w2_ref, o_ref):
    x = x_ref[0]
    mod = mod_ref[0, 0]
    sh2 = mod[3:4]
    sc2 = mod[4:5]
    gt2 = mod[5:6]
    h = (_rms(x) * (gpre_ref[...] * (1.0 + sc2)) + sh2).astype(BF16)
    y = None
    for c in range(D_FF // D_MODEL):
        cols = slice(c * D_MODEL, (c + 1) * D_MODEL)
        a = jnp.maximum(_dot(h, w1_ref[:, cols]), 0.0)
        t = _dot((a * a).astype(BF16), w2_ref[cols, :])
        y = t if y is None else y + t
    o_ref[0] = x + gt2 * (_rms(y) * gpost_ref[...])


def _mlp_call(x, mod, gpre, gpost, w1, w2, layer):
    b, s, d = x.shape
    tm = ROW_TILE
    const2 = lambda bi, si: (0, 0)
    row3 = lambda bi, si: (bi, si, 0)
    return pl.pallas_call(
        _mlp_kernel,
        grid=(b, s // tm),
        in_specs=[
            pl.BlockSpec((1, tm, d), row3),
            pl.BlockSpec((1, 1, N_MOD, d), lambda bi, si: (layer, bi, 0, 0)),
            pl.BlockSpec((1, d), const2),
            pl.BlockSpec((1, d), const2),
            pl.BlockSpec((d, D_FF), const2),
            pl.BlockSpec((D_FF, d), const2),
        ],
        out_specs=pl.BlockSpec((1, tm, d), row3),
        out_shape=jax.ShapeDtypeStruct((b, s, d), F32),
        compiler_params=pltpu.CompilerParams(
            dimension_semantics=("arbitrary", "arbitrary"), vmem_limit_bytes=VMEM_LIMIT),
        name="relu2_mlp",
    )(x, mod, gpre, gpost, w1, w2)


def kernel(x, c, ada_w, ada_b, mix_pre_g, mix_post_g, mlp_pre_g, mlp_post_g, w_in, gmlp_ln_g, gmlp_ln_b, gmlp_ws, gmlp_bs, w_a_out, conv_w, conv_b, conv_ln_g, conv_ln_b, w_b_out, fox_bf, w_c_out, w_out, mlp_w1, mlp_w2):
    n_layers = ada_w.shape[0]
    b, s, d = x.shape
    assert d == D_MODEL and s % ROW_TILE == 0 and s % ATTN_TILE == 0
    assert ROW_TILE % CUM_CHUNK == 0 and ROW_TILE % CHUNK == 0

    mod = _ada_call(c, ada_w, ada_b).reshape(n_layers, b, N_MOD, d)

    o_glu = 2 * D_MIX
    o_qkv = o_glu + 2 * D_MIX
    o_f = o_qkv + 3 * D_MIX
    o_gate = o_f + N_HEADS

    row = lambda a: a.reshape(1, -1)
    for l in range(n_layers):
        w = w_in[l]
        wuv = w[:, :o_glu].astype(BF16)
        wglu = w[:, o_glu:o_qkv].astype(BF16)
        wqkv = w[:, o_qkv:o_f].astype(BF16)
        wf = jnp.pad(w[:, o_f:o_gate], ((0, 0), (0, LANES - N_HEADS))).astype(BF16)
        wgate = w[:, o_gate:].astype(BF16)
        bf = jnp.pad(fox_bf[l], (0, LANES - N_HEADS)).reshape(1, LANES)
        wsp = gmlp_ws[l].reshape(N_PAIRS, 2, CHUNK, CHUNK).transpose(0, 2, 1, 3)
        wsp = wsp.reshape(N_PAIRS, CHUNK, 2 * CHUNK).astype(BF16)
        bsp = jnp.repeat(gmlp_bs[l].T, GROUP_DIM, axis=1)
        cw = jnp.pad(conv_w[l], ((0, CONV_HALO - CONV_WIDTH), (0, 0)))

        ya, zb, qkv, frow = _mixer_in_call(
            x, mod, row(mix_pre_g[l]), wuv, wglu, wqkv, wf,
            row(gmlp_ln_g[l]), row(gmlp_ln_b[l]), wsp, bsp, cw, row(conv_b[l]),
            row(conv_ln_g[l]), row(conv_ln_b[l]), bf, l)
        yc = _attn_call(qkv, frow)
        x = _merge_call(
            x, mod, row(mix_pre_g[l]), row(mix_post_g[l]), ya, zb, yc, wgate,
            w_a_out[l].astype(BF16), w_b_out[l].astype(BF16), w_c_out[l].astype(BF16),
            w_out[l].astype(BF16), l)
        x = _mlp_call(x, mod, row(mlp_pre_g[l]), row(mlp_post_g[l]),
                      mlp_w1[l].astype(BF16), mlp_w2[l].astype(BF16), l)
    return x
```

```python
import functools
import math

import jax
import jax.numpy as jnp
from jax import lax
from jax.experimental import pallas as pl
from jax.experimental.pallas import tpu as pltpu

D_MODEL = 1024
D_MIX = 512
N_GROUPS = 8
CHUNK = 128
GROUP_DIM = D_MIX // N_GROUPS
CONV_WIDTH = 31
N_HEADS = 8
HEAD_DIM = D_MIX // N_HEADS
N_BRANCH = 3
D_FF = 4 * D_MODEL
N_MOD = 6
NORM_EPS = 1e-6

LANES = 128
SUBLANES = 8
N_PAIRS = D_MIX // LANES
CONV_HALO = 32
CUM_CHUNK = 256
VMEM_LIMIT = 56 * 1024 * 1024

ROW_TILE = 512
ATTN_TILE = 512
STAGE_CHUNK = 512
LOG2E = math.log2(math.e)

F32 = jnp.float32
BF16 = jnp.bfloat16


def _dot(a, b):
    return jnp.dot(a, b, preferred_element_type=F32)


def _sigmoid(x):
    return 0.5 * jnp.tanh(0.5 * x) + 0.5


def _gelu_tanh(x):
    c = math.sqrt(2.0 / math.pi)
    return 0.5 * x * (1.0 + jnp.tanh(c * (x + 0.044715 * (x * x * x))))


def _rms(x):
    return x * lax.rsqrt(jnp.mean(x * x, axis=-1, keepdims=True) + NORM_EPS)


def _layer_norm(x, g, b):
    mu = jnp.mean(x, axis=-1, keepdims=True)
    xc = x - mu
    var = jnp.mean(xc * xc, axis=-1, keepdims=True)
    return xc * lax.rsqrt(var + NORM_EPS) * g + b


def _split3(x):
    hi = x.astype(BF16)
    r = x - hi.astype(F32)
    mid = r.astype(BF16)
    lo = (r - mid.astype(F32)).astype(BF16)
    return hi, mid, lo


def _ada_kernel(c_ref, w_ref, b_ref, o_ref):
    c = c_ref[...]
    c_act = (c * _sigmoid(c)).astype(BF16)
    o_ref[0] = _dot(c_act, w_ref[0].astype(BF16)) + b_ref[0]


def _ada_call(c, ada_w, ada_b):
    n_layers, d, n_out = ada_w.shape
    b = c.shape[0]
    tn = 2 * D_MODEL
    return pl.pallas_call(
        _ada_kernel,
        grid=(n_layers, n_out // tn),
        in_specs=[
            pl.BlockSpec((b, d), lambda l, j: (0, 0)),
            pl.BlockSpec((1, d, tn), lambda l, j: (l, 0, j)),
            pl.BlockSpec((1, 1, tn), lambda l, j: (l, 0, j)),
        ],
        out_specs=pl.BlockSpec((1, b, tn), lambda l, j: (l, 0, j)),
        out_shape=jax.ShapeDtypeStruct((n_layers, b, n_out), F32),
        compiler_params=pltpu.CompilerParams(
            dimension_semantics=("arbitrary", "arbitrary"), vmem_limit_bytes=VMEM_LIMIT),
        name="ada_mod",
    )(c, ada_w, ada_b.reshape(n_layers, 1, n_out))


def _mixer_in_kernel(x_ref, mod_ref, g_ref, w_ref, wb_ref,
                     lng_ref, lnb_ref, wsp_ref, bsp_ref, cw_ref, cb_ref, clg_ref, clb_ref,
                     bf_ref,
                     ya_ref, yb_ref, qkv_ref, frow_ref,
                     zs_ref, sh_ref, carry_ref):
    tm = x_ref.shape[1]
    s_idx = pl.program_id(1)

    @pl.when(s_idx == 0)
    def _():
        zs_ref[0:CONV_HALO, :] = jnp.zeros((CONV_HALO, D_MIX), F32)
        carry_ref[...] = jnp.zeros_like(carry_ref)

    x = x_ref[0]
    mod = mod_ref[0, 0]
    sh1 = mod[0:1]
    sc1 = mod[1:2]
    h = (_rms(x) * (g_ref[...] * (1.0 + sc1)) + sh1).astype(BF16)

    uv_raw = _dot(h, w_ref[0, :, 0:2 * D_MIX])
    glu = _dot(h, w_ref[0, :, 2 * D_MIX:4 * D_MIX])

    uv = _gelu_tanh(uv_raw)
    u_a = uv[:, :D_MIX]
    v_a = _layer_norm(uv[:, D_MIX:], lng_ref[...], lnb_ref[...]).astype(BF16)

    zs_ref[CONV_HALO:CONV_HALO + tm, :] = glu[:, :D_MIX] * _sigmoid(glu[:, D_MIX:])

    qkv = _dot(h, w_ref[0, :, 4 * D_MIX:7 * D_MIX])
    scale = LOG2E / math.sqrt(HEAD_DIM)
    for p in range(3 * N_PAIRS):
        blk = qkv[:, p * LANES:(p + 1) * LANES]
        qkv_ref[0, p] = (blk * scale if p < N_PAIRS else blk).astype(BF16)
    f_raw = _dot(h, w_ref[0, :, 7 * D_MIX:7 * D_MIX + LANES]) + bf_ref[...]

    n_chunks = tm // CHUNK
    lane = lax.broadcasted_iota(jnp.int32, (CHUNK, LANES), 1)
    w_row = lax.broadcasted_iota(jnp.int32, (CHUNK, 2 * CHUNK), 0)
    w_col = lax.broadcasted_iota(jnp.int32, (CHUNK, 2 * CHUNK), 1)
    causal_w = (w_col % CHUNK) <= w_row
    zero_bf = jnp.zeros((CHUNK, LANES), BF16)
    for p in range(N_PAIRS):
        cols = slice(p * LANES, (p + 1) * LANES)
        w_pair = jnp.where(causal_w, wsp_ref[p], jnp.zeros_like(wsp_ref[p]))
        rhs = []
        for c in range(n_chunks):
            vc = v_a[c * CHUNK:(c + 1) * CHUNK, cols]
            rhs.append(jnp.concatenate(
                [jnp.where(lane < GROUP_DIM, vc, zero_bf),
                 jnp.where(lane >= GROUP_DIM, vc, zero_bf)], axis=0))
        sv = _dot(w_pair, jnp.concatenate(rhs, axis=1))
        bias = bsp_ref[:, cols]
        for c in range(n_chunks):
            rows = slice(c * CHUNK, (c + 1) * CHUNK)
            ya_ref[0, rows, cols] = (
                u_a[rows, cols] * (sv[:, c * LANES:(c + 1) * LANES] + bias)).astype(BF16)

    first = CONV_HALO - (CONV_WIDTH - 1)
    n_rows = tm + CONV_HALO - SUBLANES
    conv = []
    for c in range(N_PAIRS):
        cols = slice(c * LANES, (c + 1) * LANES)
        for r in range(1, SUBLANES):
            sh_ref[c, r - 1] = zs_ref[r:r + n_rows, cols]
        acc = jnp.broadcast_to(cb_ref[:, cols], (tm, LANES))
        for j in range(CONV_WIDTH):
            r = (first + j) % SUBLANES
            base = first + j - r
            src = (zs_ref[base:base + tm, cols] if r == 0
                   else sh_ref[c, r - 1, base:base + tm, :])
            acc = acc + cw_ref[j:j + 1, cols] * src
        conv.append(acc)
    zs_ref[0:CONV_HALO, :] = zs_ref[tm:tm + CONV_HALO, :]

    log_f = jnp.minimum(f_raw, 0.0) - jnp.log1p(jnp.exp(-jnp.abs(f_raw)))
    t_row = lax.broadcasted_iota(jnp.int32, (CUM_CHUNK, CUM_CHUNK), 0)
    t_col = lax.broadcasted_iota(jnp.int32, (CUM_CHUNK, CUM_CHUNK), 1)
    tri = jnp.where(t_col <= t_row, 1.0, 0.0).astype(BF16)
    carry = carry_ref[...]
    for c in range(tm // CUM_CHUNK):
        rows = slice(c * CUM_CHUNK, (c + 1) * CUM_CHUNK)
        hi, mid, lo = _split3(log_f[rows])
        cs = _dot(tri, hi) + _dot(tri, mid) + _dot(tri, lo) + carry
        carry = cs[CUM_CHUNK - 1:CUM_CHUNK, :]
        frow_ref[0, :, rows] = cs.T[0:N_HEADS, :]
    carry_ref[...] = carry

    zn = _layer_norm(jnp.concatenate(conv, axis=1), clg_ref[...], clb_ref[...])
    yb_ref[0] = _dot((zn * _sigmoid(zn)).astype(BF16), wb_ref[0])


def _mixer_in_call(x, mod, g, w_in, wb, lng, lnb, wsp, bsp, cw, cb, clg, clb, bf, layer):
    b, s, d = x.shape
    tm = ROW_TILE
    n_cols = 7 * D_MIX + LANES
    const2 = lambda bi, si: (0, 0)
    const3 = lambda bi, si: (0, 0, 0)
    row3 = lambda bi, si: (bi, si, 0)
    out_shapes = (
        jax.ShapeDtypeStruct((b, s, D_MIX), BF16),
        jax.ShapeDtypeStruct((b, s, d), F32),
        jax.ShapeDtypeStruct((b, 3 * N_PAIRS, s, LANES), BF16),
        jax.ShapeDtypeStruct((b, N_HEADS, s), F32),
    )
    return pl.pallas_call(
        _mixer_in_kernel,
        grid=(b, s // tm),
        in_specs=[
            pl.BlockSpec((1, tm, d), row3),
            pl.BlockSpec((1, 1, N_MOD, d), lambda bi, si: (layer, bi, 0, 0)),
            pl.BlockSpec((1, d), const2),
            pl.BlockSpec((1, d, n_cols), lambda bi, si: (layer, 0, 0)),
            pl.BlockSpec((1, D_MIX, d), lambda bi, si: (layer, 0, 0)),
            pl.BlockSpec((1, D_MIX), const2),
            pl.BlockSpec((1, D_MIX), const2),
            pl.BlockSpec((N_PAIRS, CHUNK, 2 * CHUNK), const3),
            pl.BlockSpec((CHUNK, D_MIX), const2),
            pl.BlockSpec((CONV_HALO, D_MIX), const2),
            pl.BlockSpec((1, D_MIX), const2),
            pl.BlockSpec((1, D_MIX), const2),
            pl.BlockSpec((1, D_MIX), const2),
            pl.BlockSpec((1, LANES), const2),
        ],
        out_specs=(
            pl.BlockSpec((1, tm, D_MIX), row3),
            pl.BlockSpec((1, tm, d), row3),
            pl.BlockSpec((1, 3 * N_PAIRS, tm, LANES), lambda bi, si: (bi, 0, si, 0)),
            pl.BlockSpec((1, N_HEADS, tm), lambda bi, si: (bi, 0, si)),
        ),
        out_shape=out_shapes,
        scratch_shapes=[
            pltpu.VMEM((CONV_HALO + tm, D_MIX), F32),
            pltpu.VMEM((N_PAIRS, SUBLANES - 1, tm + CONV_HALO - SUBLANES, LANES), F32),
            pltpu.VMEM((1, LANES), F32),
        ],
        compiler_params=pltpu.CompilerParams(
            dimension_semantics=("arbitrary", "arbitrary"), vmem_limit_bytes=VMEM_LIMIT),
        name="mixer_in",
    )(x, mod, g, w_in, wb, lng, lnb, wsp, bsp, cw, cb, clg, clb, bf)


def _aug_key_rows(f):
    n = f.shape[1]
    hi, mid, lo = _split3(f)
    sub = lax.broadcasted_iota(jnp.int32, (N_HEADS, n), 0)
    ones3 = jnp.where(sub < 3, 1.0, 0.0)
    return jnp.concatenate(
        [-hi.astype(F32), -mid.astype(F32), -lo.astype(F32), jnp.zeros((N_HEADS, n), F32),
         ones3, jnp.zeros((LANES - 5 * N_HEADS, n), F32)], axis=0)


def _aug_query_rows(f, head, row):
    hi, mid, lo = _split3(f)
    ones = (row == head) | (row == head + N_HEADS) | (row == head + 2 * N_HEADS)
    base = jnp.where(ones, 1.0, 0.0)
    return jnp.where(row == 4 * N_HEADS, hi.astype(F32),
                     jnp.where(row == 4 * N_HEADS + 1, mid.astype(F32),
                               jnp.where(row == 4 * N_HEADS + 2, lo.astype(F32), base)))


def _attn_kernel(q_ref, k_ref, v_ref, frow_ref, o_ref,
                 kk_ref, vta_ref, vtb_ref, qq_ref, sta_ref, stb_ref, mba_ref, mbb_ref,
                 m_ref, acca_ref, accb_ref):
    s_len = k_ref.shape[2]
    tq = q_ref.shape[2]
    tk = tq
    pair = pl.program_id(1)
    qi = pl.program_id(2)

    @pl.when(qi == 0)
    def _stage_keys_values():
        rowc = lax.broadcasted_iota(jnp.int32, (LANES, STAGE_CHUNK), 0)
        for c in range(s_len // STAGE_CHUNK):
            cols = slice(c * STAGE_CHUNK, (c + 1) * STAGE_CHUNK)
            kk_ref[cols, 0:LANES] = k_ref[0, 0, cols, :]
            aug_t = _aug_key_rows(frow_ref[0, :, cols] * LOG2E)
            kk_ref[cols, LANES:2 * LANES] = aug_t.T.astype(BF16)
            vt = v_ref[0, 0, cols, :].astype(F32).T
            vta_ref[:, cols] = jnp.where(rowc < HEAD_DIM, vt, 1.0).astype(BF16)
            vtb_ref[:, cols] = jnp.where(rowc < HEAD_DIM, 1.0, vt).astype(BF16)

    row = lax.broadcasted_iota(jnp.int32, (LANES, tq), 0)
    lo_rows = row < HEAD_DIM
    q_cols = pl.ds(pl.multiple_of(qi * tq, tq), tq)
    qt = q_ref[0, 0].astype(F32).T
    top = jnp.concatenate([jnp.where(lo_rows, qt, 0.0), jnp.where(lo_rows, 0.0, qt)], axis=1)
    aug = []
    for hh in range(2):
        head = 2 * pair + hh
        f_q = frow_ref[0, pl.ds(head, 1), q_cols] * LOG2E
        aug.append(_aug_query_rows(f_q, head, row))
    qq_ref[...] = jnp.concatenate(
        [top, jnp.concatenate(aug, axis=1)], axis=0).astype(BF16)

    m_ref[...] = jnp.full(m_ref.shape, -1e30, F32)
    acca_ref[...] = jnp.zeros_like(acca_ref)
    accb_ref[...] = jnp.zeros_like(accb_ref)

    def produce(j, buf, masked):
        st_ref, mb_ref = buf
        start = pl.multiple_of(j * tk, tk)
        st = _dot(kk_ref[pl.ds(start, tk), :], qq_ref[...])
        if masked:
            r = lax.broadcasted_iota(jnp.int32, (tk, 2 * tq), 0)
            c = lax.broadcasted_iota(jnp.int32, (tk, 2 * tq), 1)
            st = jnp.where(r <= jnp.where(c >= tq, c - tq, c), st, -jnp.inf)
        st_ref[...] = st
        mb_ref[...] = jnp.max(st, axis=0, keepdims=True)

    def consume(j, buf):
        st_ref, mb_ref = buf
        start = pl.multiple_of(j * tk, tk)
        m_old = m_ref[...]
        m_new = jnp.maximum(m_old, mb_ref[...])
        alpha = jnp.exp2(m_old - m_new)
        m_ref[...] = m_new
        pt = jnp.exp2(st_ref[...] - m_new).astype(BF16)
        acca_ref[...] = (alpha[:, :tq] * acca_ref[...]
                         + _dot(vta_ref[:, pl.ds(start, tk)], pt[:, :tq]))
        accb_ref[...] = (alpha[:, tq:] * accb_ref[...]
                         + _dot(vtb_ref[:, pl.ds(start, tk)], pt[:, tq:]))

    buf_a = (sta_ref, mba_ref)
    buf_b = (stb_ref, mbb_ref)

    @pl.when(qi == 0)
    def _only_diagonal():
        produce(0, buf_a, masked=True)
        consume(0, buf_a)

    @pl.when(qi > 0)
    def _general():
        produce(0, buf_a, masked=False)
        n_loop = (qi - 1) // 2

        def body(i, carry):
            j = 2 * i
            produce(j + 1, buf_b, masked=False)
            consume(j, buf_a)
            produce(j + 2, buf_a, masked=False)
            consume(j + 1, buf_b)
            return carry

        lax.fori_loop(0, n_loop, body, 0)
        done = 2 * n_loop

        @pl.when(qi - done == 1)
        def _tail_two():
            produce(qi, buf_b, masked=True)
            consume(done, buf_a)
            consume(qi, buf_b)

        @pl.when(qi - done == 2)
        def _tail_three():
            produce(done + 1, buf_b, masked=False)
            consume(done, buf_a)
            produce(qi, buf_a, masked=True)
            consume(done + 1, buf_b)
            consume(qi, buf_a)

    inv_a = 1.0 / acca_ref[HEAD_DIM:HEAD_DIM + 1, :]
    inv_b = 1.0 / accb_ref[0:1, :]
    ot = jnp.where(lo_rows, acca_ref[...] * inv_a, accb_ref[...] * inv_b)
    o_ref[0, 0] = ot.T.astype(BF16)


def _attn_call(qkv, frow):
    b, _, s, _ = qkv.shape
    tq = ATTN_TILE
    return pl.pallas_call(
        _attn_kernel,
        grid=(b, N_PAIRS, s // tq),
        in_specs=[
            pl.BlockSpec((1, 1, tq, LANES), lambda bi, p, i: (bi, p, i, 0)),
            pl.BlockSpec((1, 1, s, LANES), lambda bi, p, i: (bi, N_PAIRS + p, 0, 0)),
            pl.BlockSpec((1, 1, s, LANES), lambda bi, p, i: (bi, 2 * N_PAIRS + p, 0, 0)),
            pl.BlockSpec((1, N_HEADS, s), lambda bi, p, i: (bi, 0, 0)),
        ],
        out_specs=pl.BlockSpec((1, 1, tq, LANES), lambda bi, p, i: (bi, p, i, 0)),
        out_shape=jax.ShapeDtypeStruct((b, N_PAIRS, s, LANES), BF16),
        scratch_shapes=[
            pltpu.VMEM((s, 2 * LANES), BF16),
            pltpu.VMEM((LANES, s), BF16),
            pltpu.VMEM((LANES, s), BF16),
            pltpu.VMEM((2 * LANES, 2 * tq), BF16),
            pltpu.VMEM((tq, 2 * tq), F32),
            pltpu.VMEM((tq, 2 * tq), F32),
            pltpu.VMEM((1, 2 * tq), F32),
            pltpu.VMEM((1, 2 * tq), F32),
            pltpu.VMEM((1, 2 * tq), F32),
            pltpu.VMEM((LANES, tq), F32),
            pltpu.VMEM((LANES, tq), F32),
        ],
        compiler_params=pltpu.CompilerParams(
            dimension_semantics=("arbitrary", "arbitrary", "arbitrary"),
            vmem_limit_bytes=VMEM_LIMIT),
        name="fox_attn",
    )(qkv, qkv, qkv, frow)


def _merge_kernel(x_ref, mod_ref, gpre_ref, gpost_ref, ya_ref, yb_ref, yc_ref,
                  wgate_ref, wa_ref, wc_ref, wout_ref, o_ref):
    x = x_ref[0]
    mod = mod_ref[0, 0]
    sh1 = mod[0:1]
    sc1 = mod[1:2]
    gt1 = mod[2:3]
    h = (_rms(x) * (gpre_ref[...] * (1.0 + sc1)) + sh1).astype(BF16)

    def gate(br):
        return _sigmoid(_dot(h, wgate_ref[0, :, br * D_MODEL:(br + 1) * D_MODEL]))

    merged = (gate(0) * _dot(ya_ref[0], wa_ref[0]) + gate(1) * yb_ref[0]
              + gate(2) * _dot(jnp.concatenate([yc_ref[0, p] for p in range(N_PAIRS)], axis=1),
                               wc_ref[0]))
    y = _dot(merged.astype(BF16), wout_ref[0])
    o_ref[0] = x + gt1 * (_rms(y) * gpost_ref[...])


def _merge_call(x, mod, gpre, gpost, ya, yb, yc, wgate, wa, wc, wout, layer):
    b, s, d = x.shape
    tm = ROW_TILE
    const2 = lambda bi, si: (0, 0)
    lay3 = lambda bi, si: (layer, 0, 0)
    row3 = lambda bi, si: (bi, si, 0)
    return pl.pallas_call(
        _merge_kernel,
        grid=(b, s // tm),
        in_specs=[
            pl.BlockSpec((1, tm, d), row3),
            pl.BlockSpec((1, 1, N_MOD, d), lambda bi, si: (layer, bi, 0, 0)),
            pl.BlockSpec((1, d), const2),
            pl.BlockSpec((1, d), const2),
            pl.BlockSpec((1, tm, D_MIX), row3),
            pl.BlockSpec((1, tm, d), row3),
            pl.BlockSpec((1, N_PAIRS, tm, LANES), lambda bi, si: (bi, 0, si, 0)),
            pl.BlockSpec((1, d, N_BRANCH * d), lay3),
            pl.BlockSpec((1, D_MIX, d), lay3),
            pl.BlockSpec((1, D_MIX, d), lay3),
            pl.BlockSpec((1, d, d), lay3),
        ],
        out_specs=pl.BlockSpec((1, tm, d), row3),
        out_shape=jax.ShapeDtypeStruct((b, s, d), F32),
        compiler_params=pltpu.CompilerParams(
            dimension_semantics=("arbitrary", "arbitrary"), vmem_limit_bytes=VMEM_LIMIT),
        name="mixer_merge",
    )(x, mod, gpre, gpost, ya, yb, yc, wgate, wa, wc, wout)


def _mlp_kernel(x_ref, mod_ref, gpre_ref, gpost_ref, w1_ref, w2_ref, o_ref):
    x = x_ref[0]
    mod = mod_ref[0, 0]
    sh2 = mod[3:4]
    sc2 = mod[4:5]
    gt2 = mod[5:6]
    h = (_rms(x) * (gpre_ref[...] * (1.0 + sc2)) + sh2).astype(BF16)
    y = None
    for c in range(D_FF // D_MODEL):
        cols = slice(c * D_MODEL, (c + 1) * D_MODEL)
        a = jnp.maximum(_dot(h, w1_ref[0, :, cols]), 0.0)
        t = _dot((a * a).astype(BF16), w2_ref[0, cols, :])
        y = t if y is None else y + t
    o_ref[0] = x + gt2 * (_rms(y) * gpost_ref[...])


def _mlp_call(x, mod, gpre, gpost, w1, w2, layer):
    b, s, d = x.shape
    tm = ROW_TILE
    const2 = lambda bi, si: (0, 0)
    lay3 = lambda bi, si: (layer, 0, 0)
    row3 = lambda bi, si: (bi, si, 0)
    return pl.pallas_call(
        _mlp_kernel,
        grid=(b, s // tm),
        in_specs=[
            pl.BlockSpec((1, tm, d), row3),
            pl.BlockSpec((1, 1, N_MOD, d), lambda bi, si: (layer, bi, 0, 0)),
            pl.BlockSpec((1, d), const2),
            pl.BlockSpec((1, d), const2),
            pl.BlockSpec((1, d, D_FF), lay3),
            pl.BlockSpec((1, D_FF, d), lay3),
        ],
        out_specs=pl.BlockSpec((1, tm, d), row3),
        out_shape=jax.ShapeDtypeStruct((b, s, d), F32),
        compiler_params=pltpu.CompilerParams(
            dimension_semantics=("arbitrary", "arbitrary"), vmem_limit_bytes=VMEM_LIMIT),
        name="relu2_mlp",
    )(x, mod, gpre, gpost, w1, w2)


def kernel(x, c, ada_w, ada_b, mix_pre_g, mix_post_g, mlp_pre_g, mlp_post_g, w_in, gmlp_ln_g, gmlp_ln_b, gmlp_ws, gmlp_bs, w_a_out, conv_w, conv_b, conv_ln_g, conv_ln_b, w_b_out, fox_bf, w_c_out, w_out, mlp_w1, mlp_w2):
    n_layers = ada_w.shape[0]
    b, s, d = x.shape
    assert d == D_MODEL and s % ROW_TILE == 0 and s % ATTN_TILE == 0
    assert ROW_TILE % CUM_CHUNK == 0 and ROW_TILE % CHUNK == 0

    mod = _ada_call(c, ada_w, ada_b).reshape(n_layers, b, N_MOD, d)

    o_gate = 7 * D_MIX + N_HEADS
    w_in_b = w_in.astype(BF16)
    w_gate_b = w_in[:, :, o_gate:].astype(BF16)
    w_a_b, w_b_b, w_c_b = w_a_out.astype(BF16), w_b_out.astype(BF16), w_c_out.astype(BF16)
    w_out_b, w1_b, w2_b = w_out.astype(BF16), mlp_w1.astype(BF16), mlp_w2.astype(BF16)

    row = lambda a: a.reshape(1, -1)
    for l in range(n_layers):
        bf = jnp.pad(fox_bf[l], (0, LANES - N_HEADS)).reshape(1, LANES)
        wsp = gmlp_ws[l].reshape(N_PAIRS, 2, CHUNK, CHUNK).transpose(0, 2, 1, 3)
        wsp = wsp.reshape(N_PAIRS, CHUNK, 2 * CHUNK).astype(BF16)
        bsp = jnp.repeat(gmlp_bs[l].T, GROUP_DIM, axis=1)
        cw = jnp.pad(conv_w[l], ((0, CONV_HALO - CONV_WIDTH), (0, 0)))

        ya, yb, qkv, frow = _mixer_in_call(
            x, mod, row(mix_pre_g[l]), w_in_b, w_b_b,
            row(gmlp_ln_g[l]), row(gmlp_ln_b[l]), wsp, bsp, cw, row(conv_b[l]),
            row(conv_ln_g[l]), row(conv_ln_b[l]), bf, l)
        yc = _attn_call(qkv, frow)
        x = _merge_call(
            x, mod, row(mix_pre_g[l]), row(mix_post_g[l]), ya, yb, yc, w_gate_b,
            w_a_b, w_c_b, w_out_b, l)
        x = _mlp_call(x, mod, row(mlp_pre_g[l]), row(mlp_post_g[l]), w1_b, w2_b, l)
    return x
```

```python
import math

import jax
import jax.numpy as jnp
from jax import lax
from jax.experimental import pallas as pl
from jax.experimental.pallas import tpu as pltpu

D_MODEL = 1024
D_MIX = 512
N_GROUPS = 8
CHUNK = 128
GROUP_DIM = D_MIX // N_GROUPS
CONV_WIDTH = 31
N_HEADS = 8
HEAD_DIM = D_MIX // N_HEADS
N_BRANCH = 3
D_FF = 4 * D_MODEL
N_MOD = 6
NORM_EPS = 1e-6

LANES = 128
SUBLANES = 8
N_PAIRS = D_MIX // LANES
CONV_HALO = 32
CUM_CHUNK = 256
VMEM_LIMIT = 56 * 1024 * 1024

ROW_TILE = 512
ATTN_TILE = 512
STAGE_CHUNK = 512
LOG2E = math.log2(math.e)

F32 = jnp.float32
BF16 = jnp.bfloat16


def _dot(a, b):
    return jnp.dot(a, b, preferred_element_type=F32)


def _sigmoid(x):
    return 0.5 * jnp.tanh(0.5 * x) + 0.5


def _gelu_tanh(x):
    c = math.sqrt(2.0 / math.pi)
    return 0.5 * x * (1.0 + jnp.tanh(c * (x + 0.044715 * (x * x * x))))


def _rms(x):
    return x * lax.rsqrt(jnp.mean(x * x, axis=-1, keepdims=True) + NORM_EPS)


def _layer_norm(x, g, b):
    mu = jnp.mean(x, axis=-1, keepdims=True)
    xc = x - mu
    var = jnp.mean(xc * xc, axis=-1, keepdims=True)
    return xc * lax.rsqrt(var + NORM_EPS) * g + b


def _split3(x):
    hi = x.astype(BF16)
    r = x - hi.astype(F32)
    mid = r.astype(BF16)
    lo = (r - mid.astype(F32)).astype(BF16)
    return hi, mid, lo


def _ada_kernel(c_ref, w_ref, b_ref, o_ref):
    c = c_ref[...]
    c_act = (c * _sigmoid(c)).astype(BF16)
    o_ref[0] = _dot(c_act, w_ref[0].astype(BF16)) + b_ref[0]


def _ada_call(c, ada_w, ada_b):
    n_layers, d, n_out = ada_w.shape
    b = c.shape[0]
    tn = 2 * D_MODEL
    return pl.pallas_call(
        _ada_kernel,
        grid=(n_layers, n_out // tn),
        in_specs=[
            pl.BlockSpec((b, d), lambda l, j: (0, 0)),
            pl.BlockSpec((1, d, tn), lambda l, j: (l, 0, j)),
            pl.BlockSpec((1, 1, tn), lambda l, j: (l, 0, j)),
        ],
        out_specs=pl.BlockSpec((1, b, tn), lambda l, j: (l, 0, j)),
        out_shape=jax.ShapeDtypeStruct((n_layers, b, n_out), F32),
        compiler_params=pltpu.CompilerParams(
            dimension_semantics=("arbitrary", "arbitrary"), vmem_limit_bytes=VMEM_LIMIT),
        name="ada_mod",
    )(c, ada_w, ada_b.reshape(n_layers, 1, n_out))


def _mixer_in_kernel(x_ref, mod_ref, g_ref, w_ref, wb_ref,
                     lng_ref, lnb_ref, wsp_ref, bsp_ref, cw_ref, cb_ref, clg_ref, clb_ref,
                     bf_ref,
                     ya_ref, yb_ref, qkv_ref, frow_ref,
                     zs_ref, sh_ref, carry_ref):
    tm = x_ref.shape[1]
    s_idx = pl.program_id(1)

    @pl.when(s_idx == 0)
    def _():
        zs_ref[0:CONV_HALO, :] = jnp.zeros((CONV_HALO, D_MIX), F32)
        carry_ref[...] = jnp.zeros_like(carry_ref)

    x = x_ref[0]
    mod = mod_ref[0, 0]
    sh1 = mod[0:1]
    sc1 = mod[1:2]
    h = (_rms(x) * (g_ref[...] * (1.0 + sc1)) + sh1).astype(BF16)

    uv_raw = _dot(h, w_ref[0, :, 0:2 * D_MIX])
    glu = _dot(h, w_ref[0, :, 2 * D_MIX:4 * D_MIX])

    uv = _gelu_tanh(uv_raw)
    u_a = uv[:, :D_MIX]
    v_a = _layer_norm(uv[:, D_MIX:], lng_ref[...], lnb_ref[...]).astype(BF16)

    zs_ref[CONV_HALO:CONV_HALO + tm, :] = glu[:, :D_MIX] * _sigmoid(glu[:, D_MIX:])

    qkv = _dot(h, w_ref[0, :, 4 * D_MIX:7 * D_MIX])
    scale = LOG2E / math.sqrt(HEAD_DIM)
    for p in range(3 * N_PAIRS):
        blk = qkv[:, p * LANES:(p + 1) * LANES]
        qkv_ref[0, p] = (blk * scale if p < N_PAIRS else blk).astype(BF16)
    f_raw = _dot(h, w_ref[0, :, 7 * D_MIX:7 * D_MIX + LANES]) + bf_ref[...]

    n_chunks = tm // CHUNK
    lane = lax.broadcasted_iota(jnp.int32, (CHUNK, LANES), 1)
    w_row = lax.broadcasted_iota(jnp.int32, (CHUNK, 2 * CHUNK), 0)
    w_col = lax.broadcasted_iota(jnp.int32, (CHUNK, 2 * CHUNK), 1)
    causal_w = (w_col % CHUNK) <= w_row
    zero_bf = jnp.zeros((CHUNK, LANES), BF16)
    for p in range(N_PAIRS):
        cols = slice(p * LANES, (p + 1) * LANES)
        w_pair = jnp.where(causal_w, wsp_ref[p], jnp.zeros_like(wsp_ref[p]))
        rhs = []
        for c in range(n_chunks):
            vc = v_a[c * CHUNK:(c + 1) * CHUNK, cols]
            rhs.append(jnp.concatenate(
                [jnp.where(lane < GROUP_DIM, vc, zero_bf),
                 jnp.where(lane >= GROUP_DIM, vc, zero_bf)], axis=0))
        sv = _dot(w_pair, jnp.concatenate(rhs, axis=1))
        bias = bsp_ref[:, cols]
        for c in range(n_chunks):
            rows = slice(c * CHUNK, (c + 1) * CHUNK)
            ya_ref[0, rows, cols] = (
                u_a[rows, cols] * (sv[:, c * LANES:(c + 1) * LANES] + bias)).astype(BF16)

    first = CONV_HALO - (CONV_WIDTH - 1)
    n_rows = tm + CONV_HALO - SUBLANES
    conv = []
    for c in range(N_PAIRS):
        cols = slice(c * LANES, (c + 1) * LANES)
        for r in range(1, SUBLANES):
            sh_ref[c, r - 1] = zs_ref[r:r + n_rows, cols]
        acc = jnp.broadcast_to(cb_ref[:, cols], (tm, LANES))
        for j in range(CONV_WIDTH):
            r = (first + j) % SUBLANES
            base = first + j - r
            src = (zs_ref[base:base + tm, cols] if r == 0
                   else sh_ref[c, r - 1, base:base + tm, :])
            acc = acc + cw_ref[j:j + 1, cols] * src
        conv.append(acc)
    zs_ref[0:CONV_HALO, :] = zs_ref[tm:tm + CONV_HALO, :]

    log_f = jnp.minimum(f_raw, 0.0) - jnp.log1p(jnp.exp(-jnp.abs(f_raw)))
    t_row = lax.broadcasted_iota(jnp.int32, (CUM_CHUNK, CUM_CHUNK), 0)
    t_col = lax.broadcasted_iota(jnp.int32, (CUM_CHUNK, CUM_CHUNK), 1)
    tri = jnp.where(t_col <= t_row, 1.0, 0.0).astype(BF16)
    carry = carry_ref[...]
    for c in range(tm // CUM_CHUNK):
        rows = slice(c * CUM_CHUNK, (c + 1) * CUM_CHUNK)
        hi, mid, lo = _split3(log_f[rows])
        cs = _dot(tri, hi) + _dot(tri, mid) + _dot(tri, lo) + carry
        carry = cs[CUM_CHUNK - 1:CUM_CHUNK, :]
        frow_ref[0, :, rows] = cs.T[0:N_HEADS, :]
    carry_ref[...] = carry

    zn = _layer_norm(jnp.concatenate(conv, axis=1), clg_ref[...], clb_ref[...])
    yb_ref[0] = _dot((zn * _sigmoid(zn)).astype(BF16), wb_ref[0])


def _mixer_in_call(x, mod, g, w_in, wb, lng, lnb, wsp, bsp, cw, cb, clg, clb, bf, layer):
    b, s, d = x.shape
    tm = ROW_TILE
    n_cols = 7 * D_MIX + LANES
    const2 = lambda bi, si: (0, 0)
    const3 = lambda bi, si: (0, 0, 0)
    row3 = lambda bi, si: (bi, si, 0)
    out_shapes = (
        jax.ShapeDtypeStruct((b, s, D_MIX), BF16),
        jax.ShapeDtypeStruct((b, s, d), F32),
        jax.ShapeDtypeStruct((b, 3 * N_PAIRS, s, LANES), BF16),
        jax.ShapeDtypeStruct((b, N_HEADS, s), F32),
    )
    return pl.pallas_call(
        _mixer_in_kernel,
        grid=(b, s // tm),
        in_specs=[
            pl.BlockSpec((1, tm, d), row3),
            pl.BlockSpec((1, 1, N_MOD, d), lambda bi, si: (layer, bi, 0, 0)),
            pl.BlockSpec((1, d), const2),
            pl.BlockSpec((1, d, n_cols), lambda bi, si: (layer, 0, 0)),
            pl.BlockSpec((1, D_MIX, d), lambda bi, si: (layer, 0, 0)),
            pl.BlockSpec((1, D_MIX), const2),
            pl.BlockSpec((1, D_MIX), const2),
            pl.BlockSpec((N_PAIRS, CHUNK, 2 * CHUNK), const3),
            pl.BlockSpec((CHUNK, D_MIX), const2),
            pl.BlockSpec((CONV_HALO, D_MIX), const2),
            pl.BlockSpec((1, D_MIX), const2),
            pl.BlockSpec((1, D_MIX), const2),
            pl.BlockSpec((1, D_MIX), const2),
            pl.BlockSpec((1, LANES), const2),
        ],
        out_specs=(
            pl.BlockSpec((1, tm, D_MIX), row3),
            pl.BlockSpec((1, tm, d), row3),
            pl.BlockSpec((1, 3 * N_PAIRS, tm, LANES), lambda bi, si: (bi, 0, si, 0)),
            pl.BlockSpec((1, N_HEADS, tm), lambda bi, si: (bi, 0, si)),
        ),
        out_shape=out_shapes,
        scratch_shapes=[
            pltpu.VMEM((CONV_HALO + tm, D_MIX), F32),
            pltpu.VMEM((N_PAIRS, SUBLANES - 1, tm + CONV_HALO - SUBLANES, LANES), F32),
            pltpu.VMEM((1, LANES), F32),
        ],
        compiler_params=pltpu.CompilerParams(
            dimension_semantics=("arbitrary", "arbitrary"), vmem_limit_bytes=VMEM_LIMIT),
        name="mixer_in",
    )(x, mod, g, w_in, wb, lng, lnb, wsp, bsp, cw, cb, clg, clb, bf)


def _aug_key_rows(f):
    n = f.shape[1]
    hi, mid, lo = _split3(f)
    sub = lax.broadcasted_iota(jnp.int32, (N_HEADS, n), 0)
    ones3 = jnp.where(sub < 3, 1.0, 0.0)
    return jnp.concatenate(
        [-hi.astype(F32), -mid.astype(F32), -lo.astype(F32), jnp.zeros((N_HEADS, n), F32),
         ones3, jnp.zeros((LANES - 5 * N_HEADS, n), F32)], axis=0)


def _aug_query_rows(f, head, row):
    hi, mid, lo = _split3(f)
    ones = (row == head) | (row == head + N_HEADS) | (row == head + 2 * N_HEADS)
    base = jnp.where(ones, 1.0, 0.0)
    return jnp.where(row == 4 * N_HEADS, hi.astype(F32),
                     jnp.where(row == 4 * N_HEADS + 1, mid.astype(F32),
                               jnp.where(row == 4 * N_HEADS + 2, lo.astype(F32), base)))


def _attn_kernel(q_ref, k_ref, v_ref, frow_ref, o_ref,
                 kk_ref, vta_ref, vtb_ref, qq_ref, sta_ref, stb_ref, mba_ref, mbb_ref,
                 m_ref, acca_ref, accb_ref):
    s_len = k_ref.shape[2]
    tq = tk = ATTN_TILE
    pair = pl.program_id(1)

    rowc = lax.broadcasted_iota(jnp.int32, (LANES, STAGE_CHUNK), 0)
    for c in range(s_len // STAGE_CHUNK):
        cols = slice(c * STAGE_CHUNK, (c + 1) * STAGE_CHUNK)
        kk_ref[cols, 0:LANES] = k_ref[0, 0, cols, :]
        aug_t = _aug_key_rows(frow_ref[0, :, cols] * LOG2E)
        kk_ref[cols, LANES:2 * LANES] = aug_t.T.astype(BF16)
        vt = v_ref[0, 0, cols, :].astype(F32).T
        vta_ref[:, cols] = jnp.where(rowc < HEAD_DIM, vt, 1.0).astype(BF16)
        vtb_ref[:, cols] = jnp.where(rowc < HEAD_DIM, 1.0, vt).astype(BF16)

    row = lax.broadcasted_iota(jnp.int32, (LANES, tq), 0)
    lo_rows = row < HEAD_DIM

    def produce(j, buf, masked):
        st_ref, mb_ref = buf
        start = pl.multiple_of(j * tk, tk)
        st = _dot(kk_ref[pl.ds(start, tk), :], qq_ref[...])
        if masked:
            r = lax.broadcasted_iota(jnp.int32, (tk, 2 * tq), 0)
            c = lax.broadcasted_iota(jnp.int32, (tk, 2 * tq), 1)
            st = jnp.where(r <= jnp.where(c >= tq, c - tq, c), st, -jnp.inf)
        st_ref[...] = st
        mb_ref[...] = jnp.max(st, axis=0, keepdims=True)

    def consume(j, buf):
        st_ref, mb_ref = buf
        start = pl.multiple_of(j * tk, tk)
        m_old = m_ref[...]
        m_new = jnp.maximum(m_old, mb_ref[...])
        alpha = jnp.exp2(m_old - m_new)
        m_ref[...] = m_new
        pt = jnp.exp2(st_ref[...] - m_new).astype(BF16)
        acca_ref[...] = (alpha[:, :tq] * acca_ref[...]
                         + _dot(vta_ref[:, pl.ds(start, tk)], pt[:, :tq]))
        accb_ref[...] = (alpha[:, tq:] * accb_ref[...]
                         + _dot(vtb_ref[:, pl.ds(start, tk)], pt[:, tq:]))

    buf_a = (sta_ref, mba_ref)
    buf_b = (stb_ref, mbb_ref)

    def q_tile(qi, carry):
        q_cols = pl.ds(pl.multiple_of(qi * tq, tq), tq)
        qt = q_ref[0, 0, q_cols, :].astype(F32).T
        top = jnp.concatenate([jnp.where(lo_rows, qt, 0.0), jnp.where(lo_rows, 0.0, qt)], axis=1)
        aug = []
        for hh in range(2):
            head = 2 * pair + hh
            f_q = frow_ref[0, pl.ds(head, 1), q_cols] * LOG2E
            aug.append(_aug_query_rows(f_q, head, row))
        qq_ref[...] = jnp.concatenate(
            [top, jnp.concatenate(aug, axis=1)], axis=0).astype(BF16)

        m_ref[...] = jnp.full(m_ref.shape, -1e30, F32)
        acca_ref[...] = jnp.zeros_like(acca_ref)
        accb_ref[...] = jnp.zeros_like(accb_ref)

        @pl.when(qi == 0)
        def _only_diagonal():
            produce(0, buf_a, masked=True)
            consume(0, buf_a)

        @pl.when(qi > 0)
        def _general():
            produce(0, buf_a, masked=False)
            n_loop = (qi - 1) // 2

            def body(i, c):
                j = 2 * i
                produce(j + 1, buf_b, masked=False)
                consume(j, buf_a)
                produce(j + 2, buf_a, masked=False)
                consume(j + 1, buf_b)
                return c

            lax.fori_loop(0, n_loop, body, 0)
            done = 2 * n_loop

            @pl.when(qi - done == 1)
            def _tail_two():
                produce(qi, buf_b, masked=True)
                consume(done, buf_a)
                consume(qi, buf_b)

            @pl.when(qi - done == 2)
            def _tail_three():
                produce(done + 1, buf_b, masked=False)
                consume(done, buf_a)
                produce(qi, buf_a, masked=True)
                consume(done + 1, buf_b)
                consume(qi, buf_a)

        inv_a = 1.0 / acca_ref[HEAD_DIM:HEAD_DIM + 1, :]
        inv_b = 1.0 / accb_ref[0:1, :]
        ot = jnp.where(lo_rows, acca_ref[...] * inv_a, accb_ref[...] * inv_b)
        o_ref[0, 0, q_cols, :] = ot.T.astype(BF16)
        return carry

    lax.fori_loop(0, s_len // tq, q_tile, 0)


def _attn_call(qkv, frow):
    b, _, s, _ = qkv.shape
    tq = ATTN_TILE
    return pl.pallas_call(
        _attn_kernel,
        grid=(b, N_PAIRS),
        in_specs=[
            pl.BlockSpec((1, 1, s, LANES), lambda bi, p: (bi, p, 0, 0)),
            pl.BlockSpec((1, 1, s, LANES), lambda bi, p: (bi, N_PAIRS + p, 0, 0)),
            pl.BlockSpec((1, 1, s, LANES), lambda bi, p: (bi, 2 * N_PAIRS + p, 0, 0)),
            pl.BlockSpec((1, N_HEADS, s), lambda bi, p: (bi, 0, 0)),
        ],
        out_specs=pl.BlockSpec((1, 1, s, LANES), lambda bi, p: (bi, p, 0, 0)),
        out_shape=jax.ShapeDtypeStruct((b, N_PAIRS, s, LANES), BF16),
        scratch_shapes=[
            pltpu.VMEM((s, 2 * LANES), BF16),
            pltpu.VMEM((LANES, s), BF16),
            pltpu.VMEM((LANES, s), BF16),
            pltpu.VMEM((2 * LANES, 2 * tq), BF16),
            pltpu.VMEM((tq, 2 * tq), F32),
            pltpu.VMEM((tq, 2 * tq), F32),
            pltpu.VMEM((1, 2 * tq), F32),
            pltpu.VMEM((1, 2 * tq), F32),
            pltpu.VMEM((1, 2 * tq), F32),
            pltpu.VMEM((LANES, tq), F32),
            pltpu.VMEM((LANES, tq), F32),
        ],
        compiler_params=pltpu.CompilerParams(
            dimension_semantics=("arbitrary", "arbitrary"), vmem_limit_bytes=VMEM_LIMIT),
        name="fox_attn",
    )(qkv, qkv, qkv, frow)


def _merge_kernel(x_ref, mod_ref, gpre_ref, gpost_ref, ya_ref, yb_ref, yc_ref,
                  wgate_ref, wa_ref, wc_ref, wout_ref, o_ref):
    x = x_ref[0]
    mod = mod_ref[0, 0]
    sh1 = mod[0:1]
    sc1 = mod[1:2]
    gt1 = mod[2:3]
    h = (_rms(x) * (gpre_ref[...] * (1.0 + sc1)) + sh1).astype(BF16)

    def gate(br):
        return _sigmoid(_dot(h, wgate_ref[0, :, br * D_MODEL:(br + 1) * D_MODEL]))

    merged = (gate(0) * _dot(ya_ref[0], wa_ref[0]) + gate(1) * yb_ref[0]
              + gate(2) * _dot(jnp.concatenate([yc_ref[0, p] for p in range(N_PAIRS)], axis=1),
                               wc_ref[0]))
    y = _dot(merged.astype(BF16), wout_ref[0])
    o_ref[0] = x + gt1 * (_rms(y) * gpost_ref[...])


def _merge_call(x, mod, gpre, gpost, ya, yb, yc, wgate, wa, wc, wout, layer):
    b, s, d = x.shape
    tm = ROW_TILE
    const2 = lambda bi, si: (0, 0)
    lay3 = lambda bi, si: (layer, 0, 0)
    row3 = lambda bi, si: (bi, si, 0)
    return pl.pallas_call(
        _merge_kernel,
        grid=(b, s // tm),
        in_specs=[
            pl.BlockSpec((1, tm, d), row3),
            pl.BlockSpec((1, 1, N_MOD, d), lambda bi, si: (layer, bi, 0, 0)),
            pl.BlockSpec((1, d), const2),
            pl.BlockSpec((1, d), const2),
            pl.BlockSpec((1, tm, D_MIX), row3),
            pl.BlockSpec((1, tm, d), row3),
            pl.BlockSpec((1, N_PAIRS, tm, LANES), lambda bi, si: (bi, 0, si, 0)),
            pl.BlockSpec((1, d, N_BRANCH * d), lay3),
            pl.BlockSpec((1, D_MIX, d), lay3),
            pl.BlockSpec((1, D_MIX, d), lay3),
            pl.BlockSpec((1, d, d), lay3),
        ],
        out_specs=pl.BlockSpec((1, tm, d), row3),
        out_shape=jax.ShapeDtypeStruct((b, s, d), F32),
        compiler_params=pltpu.CompilerParams(
            dimension_semantics=("arbitrary", "arbitrary"), vmem_limit_bytes=VMEM_LIMIT),
        name="mixer_merge",
    )(x, mod, gpre, gpost, ya, yb, yc, wgate, wa, wc, wout)


def _mlp_kernel(x_ref, mod_ref, gpre_ref, gpost_ref, w1_ref, w2_ref, o_ref):
    x = x_ref[0]
    mod = mod_ref[0, 0]
    sh2 = mod[3:4]
    sc2 = mod[4:5]
    gt2 = mod[5:6]
    h = (_rms(x) * (gpre_ref[...] * (1.0 + sc2)) + sh2).astype(BF16)
    y = None
    for c in range(D_FF // D_MODEL):
        cols = slice(c * D_MODEL, (c + 1) * D_MODEL)
        a = jnp.maximum(_dot(h, w1_ref[0, :, cols]), 0.0)
        t = _dot((a * a).astype(BF16), w2_ref[0, cols, :])
        y = t if y is None else y + t
    o_ref[0] = x + gt2 * (_rms(y) * gpost_ref[...])


def _mlp_call(x, mod, gpre, gpost, w1, w2, layer):
    b, s, d = x.shape
    tm = ROW_TILE
    const2 = lambda bi, si: (0, 0)
    lay3 = lambda bi, si: (layer, 0, 0)
    row3 = lambda bi, si: (bi, si, 0)
    return pl.pallas_call(
        _mlp_kernel,
        grid=(b, s // tm),
        in_specs=[
            pl.BlockSpec((1, tm, d), row3),
            pl.BlockSpec((1, 1, N_MOD, d), lambda bi, si: (layer, bi, 0, 0)),
            pl.BlockSpec((1, d), const2),
            pl.BlockSpec((1, d), const2),
            pl.BlockSpec((1, d, D_FF), lay3),
            pl.BlockSpec((1, D_FF, d), lay3),
        ],
        out_specs=pl.BlockSpec((1, tm, d), row3),
        out_shape=jax.ShapeDtypeStruct((b, s, d), F32),
        compiler_params=pltpu.CompilerParams(
            dimension_semantics=("arbitrary", "arbitrary"), vmem_limit_bytes=VMEM_LIMIT),
        name="relu2_mlp",
    )(x, mod, gpre, gpost, w1, w2)


def kernel(x, c, ada_w, ada_b, mix_pre_g, mix_post_g, mlp_pre_g, mlp_post_g, w_in, gmlp_ln_g, gmlp_ln_b, gmlp_ws, gmlp_bs, w_a_out, conv_w, conv_b, conv_ln_g, conv_ln_b, w_b_out, fox_bf, w_c_out, w_out, mlp_w1, mlp_w2):
    n_layers = ada_w.shape[0]
    b, s, d = x.shape
    assert d == D_MODEL and s % ROW_TILE == 0 and s % ATTN_TILE == 0
    assert ROW_TILE % CUM_CHUNK == 0 and ROW_TILE % CHUNK == 0

    mod = _ada_call(c, ada_w, ada_b).reshape(n_layers, b, N_MOD, d)

    o_gate = 7 * D_MIX + N_HEADS
    w_in_b = w_in.astype(BF16)
    w_gate_b = w_in[:, :, o_gate:].astype(BF16)
    w_a_b, w_b_b, w_c_b = w_a_out.astype(BF16), w_b_out.astype(BF16), w_c_out.astype(BF16)
    w_out_b, w1_b, w2_b = w_out.astype(BF16), mlp_w1.astype(BF16), mlp_w2.astype(BF16)

    row = lambda a: a.reshape(1, -1)
    for l in range(n_layers):
        bf = jnp.pad(fox_bf[l], (0, LANES - N_HEADS)).reshape(1, LANES)
        wsp = gmlp_ws[l].reshape(N_PAIRS, 2, CHUNK, CHUNK).transpose(0, 2, 1, 3)
        wsp = wsp.reshape(N_PAIRS, CHUNK, 2 * CHUNK).astype(BF16)
        bsp = jnp.repeat(gmlp_bs[l].T, GROUP_DIM, axis=1)
        cw = jnp.pad(conv_w[l], ((0, CONV_HALO - CONV_WIDTH), (0, 0)))

        ya, yb, qkv, frow = _mixer_in_call(
            x, mod, row(mix_pre_g[l]), w_in_b, w_b_b,
            row(gmlp_ln_g[l]), row(gmlp_ln_b[l]), wsp, bsp, cw, row(conv_b[l]),
            row(conv_ln_g[l]), row(conv_ln_b[l]), bf, l)
        yc = _attn_call(qkv, frow)
        x = _merge_call(
            x, mod, row(mix_pre_g[l]), row(mix_post_g[l]), ya, yb, yc, w_gate_b,
            w_a_b, w_c_b, w_out_b, l)
        x = _mlp_call(x, mod, row(mlp_pre_g[l]), row(mlp_post_g[l]), w1_b, w2_b, l)
    return x
```

```python
import math

import jax
import jax.numpy as jnp
from jax import lax
from jax.experimental import pallas as pl
from jax.experimental.pallas import tpu as pltpu

D_MODEL = 1024
D_MIX = 512
N_GROUPS = 8
CHUNK = 128
GROUP_DIM = D_MIX // N_GROUPS
CONV_WIDTH = 31
N_HEADS = 8
HEAD_DIM = D_MIX // N_HEADS
N_BRANCH = 3
D_FF = 4 * D_MODEL
N_MOD = 6
NORM_EPS = 1e-6

LANES = 128
SUBLANES = 8
N_PAIRS = D_MIX // LANES
CONV_HALO = 32
CUM_CHUNK = 256
VMEM_LIMIT = 56 * 1024 * 1024

ROW_TILE = 512
ATTN_TILE = 512
STAGE_CHUNK = 512
V_ROWS = HEAD_DIM + 16
LOG2E = math.log2(math.e)

F32 = jnp.float32
BF16 = jnp.bfloat16


def _dot(a, b):
    return jnp.dot(a, b, preferred_element_type=F32)


def _sigmoid(x):
    return 0.5 * jnp.tanh(0.5 * x) + 0.5


def _gelu_tanh(x):
    c = math.sqrt(2.0 / math.pi)
    return 0.5 * x * (1.0 + jnp.tanh(c * (x + 0.044715 * (x * x * x))))


def _rms(x):
    return x * lax.rsqrt(jnp.mean(x * x, axis=-1, keepdims=True) + NORM_EPS)


def _layer_norm(x, g, b):
    mu = jnp.mean(x, axis=-1, keepdims=True)
    xc = x - mu
    var = jnp.mean(xc * xc, axis=-1, keepdims=True)
    return xc * lax.rsqrt(var + NORM_EPS) * g + b


def _split3(x):
    hi = x.astype(BF16)
    r = x - hi.astype(F32)
    mid = r.astype(BF16)
    lo = (r - mid.astype(F32)).astype(BF16)
    return hi, mid, lo


def _ada_kernel(c_ref, w_ref, b_ref, o_ref):
    c = c_ref[...]
    c_act = (c * _sigmoid(c)).astype(BF16)
    o_ref[0] = _dot(c_act, w_ref[0].astype(BF16)) + b_ref[0]


def _ada_call(c, ada_w, ada_b):
    n_layers, d, n_out = ada_w.shape
    b = c.shape[0]
    tn = 2 * D_MODEL
    return pl.pallas_call(
        _ada_kernel,
        grid=(n_layers, n_out // tn),
        in_specs=[
            pl.BlockSpec((b, d), lambda l, j: (0, 0)),
            pl.BlockSpec((1, d, tn), lambda l, j: (l, 0, j)),
            pl.BlockSpec((1, 1, tn), lambda l, j: (l, 0, j)),
        ],
        out_specs=pl.BlockSpec((1, b, tn), lambda l, j: (l, 0, j)),
        out_shape=jax.ShapeDtypeStruct((n_layers, b, n_out), F32),
        compiler_params=pltpu.CompilerParams(
            dimension_semantics=("arbitrary", "arbitrary"), vmem_limit_bytes=VMEM_LIMIT),
        name="ada_mod",
    )(c, ada_w, ada_b.reshape(n_layers, 1, n_out))


def _mixer_in_kernel(x_ref, mod_ref, g_ref, w_ref, wb_ref,
                     lng_ref, lnb_ref, wsp_ref, bsp_ref, cw_ref, cb_ref, clg_ref, clb_ref,
                     bf_ref,
                     ya_ref, yb_ref, qkv_ref, frow_ref,
                     zs_ref, sh_ref, carry_ref):
    tm = x_ref.shape[1]
    s_idx = pl.program_id(1)

    @pl.when(s_idx == 0)
    def _():
        zs_ref[0:CONV_HALO, :] = jnp.zeros((CONV_HALO, D_MIX), F32)
        carry_ref[...] = jnp.zeros_like(carry_ref)

    x = x_ref[0]
    mod = mod_ref[0, 0]
    sh1 = mod[0:1]
    sc1 = mod[1:2]
    h = (_rms(x) * (g_ref[...] * (1.0 + sc1)) + sh1).astype(BF16)

    uv_raw = _dot(h, w_ref[0, :, 0:2 * D_MIX])
    glu = _dot(h, w_ref[0, :, 2 * D_MIX:4 * D_MIX])

    uv = _gelu_tanh(uv_raw)
    u_a = uv[:, :D_MIX]
    v_a = _layer_norm(uv[:, D_MIX:], lng_ref[...], lnb_ref[...]).astype(BF16)

    zs_ref[CONV_HALO:CONV_HALO + tm, :] = glu[:, :D_MIX] * _sigmoid(glu[:, D_MIX:])

    qkv = _dot(h, w_ref[0, :, 4 * D_MIX:7 * D_MIX])
    scale = LOG2E / math.sqrt(HEAD_DIM)
    for p in range(3 * N_PAIRS):
        blk = qkv[:, p * LANES:(p + 1) * LANES]
        qkv_ref[0, p] = (blk * scale if p < N_PAIRS else blk).astype(BF16)
    f_raw = _dot(h, w_ref[0, :, 7 * D_MIX:7 * D_MIX + LANES]) + bf_ref[...]

    n_chunks = tm // CHUNK
    lane = lax.broadcasted_iota(jnp.int32, (CHUNK, LANES), 1)
    w_row = lax.broadcasted_iota(jnp.int32, (CHUNK, 2 * CHUNK), 0)
    w_col = lax.broadcasted_iota(jnp.int32, (CHUNK, 2 * CHUNK), 1)
    causal_w = (w_col % CHUNK) <= w_row
    zero_bf = jnp.zeros((CHUNK, LANES), BF16)
    for p in range(N_PAIRS):
        cols = slice(p * LANES, (p + 1) * LANES)
        w_pair = jnp.where(causal_w, wsp_ref[p], jnp.zeros_like(wsp_ref[p]))
        rhs = []
        for c in range(n_chunks):
            vc = v_a[c * CHUNK:(c + 1) * CHUNK, cols]
            rhs.append(jnp.concatenate(
                [jnp.where(lane < GROUP_DIM, vc, zero_bf),
                 jnp.where(lane >= GROUP_DIM, vc, zero_bf)], axis=0))
        sv = _dot(w_pair, jnp.concatenate(rhs, axis=1))
        bias = bsp_ref[:, cols]
        for c in range(n_chunks):
            rows = slice(c * CHUNK, (c + 1) * CHUNK)
            ya_ref[0, rows, cols] = (
                u_a[rows, cols] * (sv[:, c * LANES:(c + 1) * LANES] + bias)).astype(BF16)

    first = CONV_HALO - (CONV_WIDTH - 1)
    n_rows = tm + CONV_HALO - SUBLANES
    conv = []
    for c in range(N_PAIRS):
        cols = slice(c * LANES, (c + 1) * LANES)
        for r in range(1, SUBLANES):
            sh_ref[c, r - 1] = zs_ref[r:r + n_rows, cols]
        acc = jnp.broadcast_to(cb_ref[:, cols], (tm, LANES))
        for j in range(CONV_WIDTH):
            r = (first + j) % SUBLANES
            base = first + j - r
            src = (zs_ref[base:base + tm, cols] if r == 0
                   else sh_ref[c, r - 1, base:base + tm, :])
            acc = acc + cw_ref[j:j + 1, cols] * src
        conv.append(acc)
    zs_ref[0:CONV_HALO, :] = zs_ref[tm:tm + CONV_HALO, :]

    log_f = jnp.minimum(f_raw, 0.0) - jnp.log1p(jnp.exp(-jnp.abs(f_raw)))
    t_row = lax.broadcasted_iota(jnp.int32, (CUM_CHUNK, CUM_CHUNK), 0)
    t_col = lax.broadcasted_iota(jnp.int32, (CUM_CHUNK, CUM_CHUNK), 1)
    tri = jnp.where(t_col <= t_row, 1.0, 0.0).astype(BF16)
    carry = carry_ref[...]
    for c in range(tm // CUM_CHUNK):
        rows = slice(c * CUM_CHUNK, (c + 1) * CUM_CHUNK)
        hi, mid, lo = _split3(log_f[rows])
        cs = _dot(tri, hi) + _dot(tri, mid) + _dot(tri, lo) + carry
        carry = cs[CUM_CHUNK - 1:CUM_CHUNK, :]
        frow_ref[0, :, rows] = cs.T[0:N_HEADS, :]
    carry_ref[...] = carry

    zn = _layer_norm(jnp.concatenate(conv, axis=1), clg_ref[...], clb_ref[...])
    yb_ref[0] = _dot((zn * _sigmoid(zn)).astype(BF16), wb_ref[0])


def _mixer_in_call(x, mod, g, w_in, wb, lng, lnb, wsp, bsp, cw, cb, clg, clb, bf, layer):
    b, s, d = x.shape
    tm = ROW_TILE
    n_cols = 7 * D_MIX + LANES
    const2 = lambda bi, si: (0, 0)
    const3 = lambda bi, si: (0, 0, 0)
    row3 = lambda bi, si: (bi, si, 0)
    out_shapes = (
        jax.ShapeDtypeStruct((b, s, D_MIX), BF16),
        jax.ShapeDtypeStruct((b, s, d), F32),
        jax.ShapeDtypeStruct((b, 3 * N_PAIRS, s, LANES), BF16),
        jax.ShapeDtypeStruct((b, N_HEADS, s), F32),
    )
    return pl.pallas_call(
        _mixer_in_kernel,
        grid=(b, s // tm),
        in_specs=[
            pl.BlockSpec((1, tm, d), row3),
            pl.BlockSpec((1, 1, N_MOD, d), lambda bi, si: (layer, bi, 0, 0)),
            pl.BlockSpec((1, d), const2),
            pl.BlockSpec((1, d, n_cols), lambda bi, si: (layer, 0, 0)),
            pl.BlockSpec((1, D_MIX, d), lambda bi, si: (layer, 0, 0)),
            pl.BlockSpec((1, D_MIX), const2),
            pl.BlockSpec((1, D_MIX), const2),
            pl.BlockSpec((N_PAIRS, CHUNK, 2 * CHUNK), const3),
            pl.BlockSpec((CHUNK, D_MIX), const2),
            pl.BlockSpec((CONV_HALO, D_MIX), const2),
            pl.BlockSpec((1, D_MIX), const2),
            pl.BlockSpec((1, D_MIX), const2),
            pl.BlockSpec((1, D_MIX), const2),
            pl.BlockSpec((1, LANES), const2),
        ],
        out_specs=(
            pl.BlockSpec((1, tm, D_MIX), row3),
            pl.BlockSpec((1, tm, d), row3),
            pl.BlockSpec((1, 3 * N_PAIRS, tm, LANES), lambda bi, si: (bi, 0, si, 0)),
            pl.BlockSpec((1, N_HEADS, tm), lambda bi, si: (bi, 0, si)),
        ),
        out_shape=out_shapes,
        scratch_shapes=[
            pltpu.VMEM((CONV_HALO + tm, D_MIX), F32),
            pltpu.VMEM((N_PAIRS, SUBLANES - 1, tm + CONV_HALO - SUBLANES, LANES), F32),
            pltpu.VMEM((1, LANES), F32),
        ],
        compiler_params=pltpu.CompilerParams(
            dimension_semantics=("arbitrary", "arbitrary"), vmem_limit_bytes=VMEM_LIMIT),
        name="mixer_in",
    )(x, mod, g, w_in, wb, lng, lnb, wsp, bsp, cw, cb, clg, clb, bf)


def _aug_key_rows(f):
    n = f.shape[1]
    hi, mid, lo = _split3(f)
    sub = lax.broadcasted_iota(jnp.int32, (N_HEADS, n), 0)
    ones3 = jnp.where(sub < 3, 1.0, 0.0)
    return jnp.concatenate(
        [-hi.astype(F32), -mid.astype(F32), -lo.astype(F32), jnp.zeros((N_HEADS, n), F32),
         ones3, jnp.zeros((LANES - 5 * N_HEADS, n), F32)], axis=0)


def _aug_query_rows(f, head, row):
    hi, mid, lo = _split3(f)
    ones = (row == head) | (row == head + N_HEADS) | (row == head + 2 * N_HEADS)
    base = jnp.where(ones, 1.0, 0.0)
    return jnp.where(row == 4 * N_HEADS, hi.astype(F32),
                     jnp.where(row == 4 * N_HEADS + 1, mid.astype(F32),
                               jnp.where(row == 4 * N_HEADS + 2, lo.astype(F32), base)))


def _attn_kernel(q_ref, k_ref, v_ref, frow_ref, o_ref,
                 kk_ref, vta_ref, vtb_ref, qq_ref, sta_ref, stb_ref, mba_ref, mbb_ref,
                 m_ref, acca_ref, accb_ref):
    s_len = k_ref.shape[2]
    tq = tk = ATTN_TILE
    pair = pl.program_id(1)

    ones_rows = jnp.ones((V_ROWS - HEAD_DIM, STAGE_CHUNK), BF16)
    for c in range(s_len // STAGE_CHUNK):
        cols = slice(c * STAGE_CHUNK, (c + 1) * STAGE_CHUNK)
        kk_ref[cols, 0:LANES] = k_ref[0, 0, cols, :]
        aug_t = _aug_key_rows(frow_ref[0, :, cols] * LOG2E)
        kk_ref[cols, LANES:2 * LANES] = aug_t.T.astype(BF16)
        vt = v_ref[0, 0, cols, :].astype(F32).T
        for vt_ref, head_rows in ((vta_ref, vt[:HEAD_DIM]), (vtb_ref, vt[HEAD_DIM:])):
            vt_ref[0:HEAD_DIM, cols] = head_rows.astype(BF16)
            vt_ref[HEAD_DIM:V_ROWS, cols] = ones_rows

    row = lax.broadcasted_iota(jnp.int32, (LANES, tq), 0)
    lo_rows = row < HEAD_DIM

    def produce(j, buf, masked):
        st_ref, mb_ref = buf
        start = pl.multiple_of(j * tk, tk)
        st = _dot(kk_ref[pl.ds(start, tk), :], qq_ref[...])
        if masked:
            r = lax.broadcasted_iota(jnp.int32, (tk, 2 * tq), 0)
            c = lax.broadcasted_iota(jnp.int32, (tk, 2 * tq), 1)
            st = jnp.where(r <= jnp.where(c >= tq, c - tq, c), st, -jnp.inf)
        st_ref[...] = st
        mb_ref[...] = jnp.max(st, axis=0, keepdims=True)

    def consume(j, buf):
        st_ref, mb_ref = buf
        start = pl.multiple_of(j * tk, tk)
        m_old = m_ref[...]
        m_new = jnp.maximum(m_old, mb_ref[...])
        alpha = jnp.exp2(m_old - m_new)
        m_ref[...] = m_new
        pt = jnp.exp2(st_ref[...] - m_new).astype(BF16)
        acca_ref[...] = (alpha[:, :tq] * acca_ref[...]
                         + _dot(vta_ref[:, pl.ds(start, tk)], pt[:, :tq]))
        accb_ref[...] = (alpha[:, tq:] * accb_ref[...]
                         + _dot(vtb_ref[:, pl.ds(start, tk)], pt[:, tq:]))

    buf_a = (sta_ref, mba_ref)
    buf_b = (stb_ref, mbb_ref)

    def q_tile(qi, carry):
        q_cols = pl.ds(pl.multiple_of(qi * tq, tq), tq)
        qt = q_ref[0, 0, q_cols, :].astype(F32).T
        top = jnp.concatenate([jnp.where(lo_rows, qt, 0.0), jnp.where(lo_rows, 0.0, qt)], axis=1)
        aug = []
        for hh in range(2):
            head = 2 * pair + hh
            f_q = frow_ref[0, pl.ds(head, 1), q_cols] * LOG2E
            aug.append(_aug_query_rows(f_q, head, row))
        qq_ref[...] = jnp.concatenate(
            [top, jnp.concatenate(aug, axis=1)], axis=0).astype(BF16)

        m_ref[...] = jnp.full(m_ref.shape, -1e30, F32)
        acca_ref[...] = jnp.zeros_like(acca_ref)
        accb_ref[...] = jnp.zeros_like(accb_ref)

        @pl.when(qi == 0)
        def _only_diagonal():
            produce(0, buf_a, masked=True)
            consume(0, buf_a)

        @pl.when(qi > 0)
        def _general():
            produce(0, buf_a, masked=False)
            n_loop = (qi - 1) // 2

            def body(i, c):
                j = 2 * i
                produce(j + 1, buf_b, masked=False)
                consume(j, buf_a)
                produce(j + 2, buf_a, masked=False)
                consume(j + 1, buf_b)
                return c

            lax.fori_loop(0, n_loop, body, 0)
            done = 2 * n_loop

            @pl.when(qi - done == 1)
            def _tail_two():
                produce(qi, buf_b, masked=True)
                consume(done, buf_a)
                consume(qi, buf_b)

            @pl.when(qi - done == 2)
            def _tail_three():
                produce(done + 1, buf_b, masked=False)
                consume(done, buf_a)
                produce(qi, buf_a, masked=True)
                consume(done + 1, buf_b)
                consume(qi, buf_a)

        ot = jnp.concatenate(
            [acc_ref[0:HEAD_DIM, :] * (1.0 / acc_ref[HEAD_DIM:HEAD_DIM + 1, :])
             for acc_ref in (acca_ref, accb_ref)], axis=0)
        o_ref[0, 0, q_cols, :] = ot.T.astype(BF16)
        return carry

    lax.fori_loop(0, s_len // tq, q_tile, 0)


def _attn_call(qkv, frow):
    b, _, s, _ = qkv.shape
    tq = ATTN_TILE
    return pl.pallas_call(
        _attn_kernel,
        grid=(b, N_PAIRS),
        in_specs=[
            pl.BlockSpec((1, 1, s, LANES), lambda bi, p: (bi, p, 0, 0)),
            pl.BlockSpec((1, 1, s, LANES), lambda bi, p: (bi, N_PAIRS + p, 0, 0)),
            pl.BlockSpec((1, 1, s, LANES), lambda bi, p: (bi, 2 * N_PAIRS + p, 0, 0)),
            pl.BlockSpec((1, N_HEADS, s), lambda bi, p: (bi, 0, 0)),
        ],
        out_specs=pl.BlockSpec((1, 1, s, LANES), lambda bi, p: (bi, p, 0, 0)),
        out_shape=jax.ShapeDtypeStruct((b, N_PAIRS, s, LANES), BF16),
        scratch_shapes=[
            pltpu.VMEM((s, 2 * LANES), BF16),
            pltpu.VMEM((V_ROWS, s), BF16),
            pltpu.VMEM((V_ROWS, s), BF16),
            pltpu.VMEM((2 * LANES, 2 * tq), BF16),
            pltpu.VMEM((tq, 2 * tq), F32),
            pltpu.VMEM((tq, 2 * tq), F32),
            pltpu.VMEM((1, 2 * tq), F32),
            pltpu.VMEM((1, 2 * tq), F32),
            pltpu.VMEM((1, 2 * tq), F32),
            pltpu.VMEM((V_ROWS, tq), F32),
            pltpu.VMEM((V_ROWS, tq), F32),
        ],
        compiler_params=pltpu.CompilerParams(
            dimension_semantics=("arbitrary", "arbitrary"), vmem_limit_bytes=VMEM_LIMIT),
        name="fox_attn",
    )(qkv, qkv, qkv, frow)


def _merge_kernel(x_ref, mod_ref, gpre_ref, gpost_ref, ya_ref, yb_ref, yc_ref,
                  wgate_ref, wa_ref, wc_ref, wout_ref, o_ref):
    x = x_ref[0]
    mod = mod_ref[0, 0]
    sh1 = mod[0:1]
    sc1 = mod[1:2]
    gt1 = mod[2:3]
    h = (_rms(x) * (gpre_ref[...] * (1.0 + sc1)) + sh1).astype(BF16)

    def gate(br):
        return _sigmoid(_dot(h, wgate_ref[0, :, br * D_MODEL:(br + 1) * D_MODEL]))

    merged = (gate(0) * _dot(ya_ref[0], wa_ref[0]) + gate(1) * yb_ref[0]
              + gate(2) * _dot(jnp.concatenate([yc_ref[0, p] for p in range(N_PAIRS)], axis=1),
                               wc_ref[0]))
    y = _dot(merged.astype(BF16), wout_ref[0])
    o_ref[0] = x + gt1 * (_rms(y) * gpost_ref[...])


def _merge_call(x, mod, gpre, gpost, ya, yb, yc, wgate, wa, wc, wout, layer):
    b, s, d = x.shape
    tm = ROW_TILE
    const2 = lambda bi, si: (0, 0)
    lay3 = lambda bi, si: (layer, 0, 0)
    row3 = lambda bi, si: (bi, si, 0)
    return pl.pallas_call(
        _merge_kernel,
        grid=(b, s // tm),
        in_specs=[
            pl.BlockSpec((1, tm, d), row3),
            pl.BlockSpec((1, 1, N_MOD, d), lambda bi, si: (layer, bi, 0, 0)),
            pl.BlockSpec((1, d), const2),
            pl.BlockSpec((1, d), const2),
            pl.BlockSpec((1, tm, D_MIX), row3),
            pl.BlockSpec((1, tm, d), row3),
            pl.BlockSpec((1, N_PAIRS, tm, LANES), lambda bi, si: (bi, 0, si, 0)),
            pl.BlockSpec((1, d, N_BRANCH * d), lay3),
            pl.BlockSpec((1, D_MIX, d), lay3),
            pl.BlockSpec((1, D_MIX, d), lay3),
            pl.BlockSpec((1, d, d), lay3),
        ],
        out_specs=pl.BlockSpec((1, tm, d), row3),
        out_shape=jax.ShapeDtypeStruct((b, s, d), F32),
        compiler_params=pltpu.CompilerParams(
            dimension_semantics=("arbitrary", "arbitrary"), vmem_limit_bytes=VMEM_LIMIT),
        name="mixer_merge",
    )(x, mod, gpre, gpost, ya, yb, yc, wgate, wa, wc, wout)


def _mlp_kernel(x_ref, mod_ref, gpre_ref, gpost_ref, w1_ref, w2_ref, o_ref):
    x = x_ref[0]
    mod = mod_ref[0, 0]
    sh2 = mod[3:4]
    sc2 = mod[4:5]
    gt2 = mod[5:6]
    h = (_rms(x) * (gpre_ref[...] * (1.0 + sc2)) + sh2).astype(BF16)
    y = None
    for c in range(D_FF // D_MODEL):
        cols = slice(c * D_MODEL, (c + 1) * D_MODEL)
        a = jnp.maximum(_dot(h, w1_ref[0, :, cols]), 0.0)
        t = _dot((a * a).astype(BF16), w2_ref[0, cols, :])
        y = t if y is None else y + t
    o_ref[0] = x + gt2 * (_rms(y) * gpost_ref[...])


def _mlp_call(x, mod, gpre, gpost, w1, w2, layer):
    b, s, d = x.shape
    tm = ROW_TILE
    const2 = lambda bi, si: (0, 0)
    lay3 = lambda bi, si: (layer, 0, 0)
    row3 = lambda bi, si: (bi, si, 0)
    return pl.pallas_call(
        _mlp_kernel,
        grid=(b, s // tm),
        in_specs=[
            pl.BlockSpec((1, tm, d), row3),
            pl.BlockSpec((1, 1, N_MOD, d), lambda bi, si: (layer, bi, 0, 0)),
            pl.BlockSpec((1, d), const2),
            pl.BlockSpec((1, d), const2),
            pl.BlockSpec((1, d, D_FF), lay3),
            pl.BlockSpec((1, D_FF, d), lay3),
        ],
        out_specs=pl.BlockSpec((1, tm, d), row3),
        out_shape=jax.ShapeDtypeStruct((b, s, d), F32),
        compiler_params=pltpu.CompilerParams(
            dimension_semantics=("arbitrary", "arbitrary"), vmem_limit_bytes=VMEM_LIMIT),
        name="relu2_mlp",
    )(x, mod, gpre, gpost, w1, w2)


def kernel(x, c, ada_w, ada_b, mix_pre_g, mix_post_g, mlp_pre_g, mlp_post_g, w_in, gmlp_ln_g, gmlp_ln_b, gmlp_ws, gmlp_bs, w_a_out, conv_w, conv_b, conv_ln_g, conv_ln_b, w_b_out, fox_bf, w_c_out, w_out, mlp_w1, mlp_w2):
    n_layers = ada_w.shape[0]
    b, s, d = x.shape
    assert d == D_MODEL and s % ROW_TILE == 0 and s % ATTN_TILE == 0
    assert ROW_TILE % CUM_CHUNK == 0 and ROW_TILE % CHUNK == 0

    mod = _ada_call(c, ada_w, ada_b).reshape(n_layers, b, N_MOD, d)

    o_gate = 7 * D_MIX + N_HEADS
    w_in_b = w_in.astype(BF16)
    w_gate_b = w_in_b[:, :, o_gate:]
    w_a_b, w_b_b, w_c_b = w_a_out.astype(BF16), w_b_out.astype(BF16), w_c_out.astype(BF16)
    w_out_b, w1_b, w2_b = w_out.astype(BF16), mlp_w1.astype(BF16), mlp_w2.astype(BF16)

    row = lambda a: a.reshape(1, -1)
    for l in range(n_layers):
        bf = jnp.pad(fox_bf[l], (0, LANES - N_HEADS)).reshape(1, LANES)
        wsp = gmlp_ws[l].reshape(N_PAIRS, 2, CHUNK, CHUNK).transpose(0, 2, 1, 3)
        wsp = wsp.reshape(N_PAIRS, CHUNK, 2 * CHUNK).astype(BF16)
        bsp = jnp.repeat(gmlp_bs[l].T, GROUP_DIM, axis=1)
        cw = jnp.pad(conv_w[l], ((0, CONV_HALO - CONV_WIDTH), (0, 0)))

        ya, yb, qkv, frow = _mixer_in_call(
            x, mod, row(mix_pre_g[l]), w_in_b, w_b_b,
            row(gmlp_ln_g[l]), row(gmlp_ln_b[l]), wsp, bsp, cw, row(conv_b[l]),
            row(conv_ln_g[l]), row(conv_ln_b[l]), bf, l)
        yc = _attn_call(qkv, frow)
        x = _merge_call(
            x, mod, row(mix_pre_g[l]), row(mix_post_g[l]), ya, yb, yc, w_gate_b,
            w_a_b, w_c_b, w_out_b, l)
        x = _mlp_call(x, mod, row(mlp_pre_g[l]), row(mlp_post_g[l]), w1_b, w2_b, l)
    return x
```

```python
import math

import jax
import jax.numpy as jnp
from jax import lax
from jax.experimental import pallas as pl
from jax.experimental.pallas import tpu as pltpu

D_MODEL = 1024
D_MIX = 512
N_GROUPS = 8
CHUNK = 128
GROUP_DIM = D_MIX // N_GROUPS
CONV_WIDTH = 31
N_HEADS = 8
HEAD_DIM = D_MIX // N_HEADS
N_BRANCH = 3
D_FF = 4 * D_MODEL
N_MOD = 6
NORM_EPS = 1e-6

LANES = 128
SUBLANES = 8
N_PAIRS = D_MIX // LANES
CONV_HALO = 32
CUM_CHUNK = 256
VMEM_LIMIT = 56 * 1024 * 1024

ROW_TILE = 512
ATTN_TILE = 1024
STAGE_CHUNK = 512
V_ROWS = HEAD_DIM + 16
LOG2E = math.log2(math.e)

F32 = jnp.float32
BF16 = jnp.bfloat16


def _dot(a, b):
    return jnp.dot(a, b, preferred_element_type=F32)


def _sigmoid(x):
    return 0.5 * jnp.tanh(0.5 * x) + 0.5


def _gelu_tanh(x):
    c = math.sqrt(2.0 / math.pi)
    return 0.5 * x * (1.0 + jnp.tanh(c * (x + 0.044715 * (x * x * x))))


def _rms(x):
    return x * lax.rsqrt(jnp.mean(x * x, axis=-1, keepdims=True) + NORM_EPS)


def _layer_norm(x, g, b):
    mu = jnp.mean(x, axis=-1, keepdims=True)
    xc = x - mu
    var = jnp.mean(xc * xc, axis=-1, keepdims=True)
    return xc * lax.rsqrt(var + NORM_EPS) * g + b


def _split3(x):
    hi = x.astype(BF16)
    r = x - hi.astype(F32)
    mid = r.astype(BF16)
    lo = (r - mid.astype(F32)).astype(BF16)
    return hi, mid, lo


def _ada_kernel(c_ref, w_ref, b_ref, o_ref):
    c = c_ref[...]
    c_act = (c * _sigmoid(c)).astype(BF16)
    o_ref[0] = _dot(c_act, w_ref[0].astype(BF16)) + b_ref[0]


def _ada_call(c, ada_w, ada_b):
    n_layers, d, n_out = ada_w.shape
    b = c.shape[0]
    tn = 2 * D_MODEL
    return pl.pallas_call(
        _ada_kernel,
        grid=(n_layers, n_out // tn),
        in_specs=[
            pl.BlockSpec((b, d), lambda l, j: (0, 0)),
            pl.BlockSpec((1, d, tn), lambda l, j: (l, 0, j)),
            pl.BlockSpec((1, 1, tn), lambda l, j: (l, 0, j)),
        ],
        out_specs=pl.BlockSpec((1, b, tn), lambda l, j: (l, 0, j)),
        out_shape=jax.ShapeDtypeStruct((n_layers, b, n_out), F32),
        compiler_params=pltpu.CompilerParams(
            dimension_semantics=("arbitrary", "arbitrary"), vmem_limit_bytes=VMEM_LIMIT),
        name="ada_mod",
    )(c, ada_w, ada_b.reshape(n_layers, 1, n_out))


def _mixer_in_kernel(x_ref, mod_ref, g_ref, w_ref, wb_ref,
                     lng_ref, lnb_ref, wsp_ref, bsp_ref, cw_ref, cb_ref, clg_ref, clb_ref,
                     bf_ref,
                     ya_ref, yb_ref, qkv_ref, frow_ref,
                     zs_ref, sh_ref, carry_ref):
    tm = x_ref.shape[1]
    s_idx = pl.program_id(1)

    @pl.when(s_idx == 0)
    def _():
        zs_ref[0:CONV_HALO, :] = jnp.zeros((CONV_HALO, D_MIX), F32)
        carry_ref[...] = jnp.zeros_like(carry_ref)

    x = x_ref[0]
    mod = mod_ref[0, 0]
    sh1 = mod[0:1]
    sc1 = mod[1:2]
    h = (_rms(x) * (g_ref[...] * (1.0 + sc1)) + sh1).astype(BF16)

    uv_raw = _dot(h, w_ref[0, :, 0:2 * D_MIX])
    glu = _dot(h, w_ref[0, :, 2 * D_MIX:4 * D_MIX])

    uv = _gelu_tanh(uv_raw)
    u_a = uv[:, :D_MIX]
    v_a = _layer_norm(uv[:, D_MIX:], lng_ref[...], lnb_ref[...]).astype(BF16)

    zs_ref[CONV_HALO:CONV_HALO + tm, :] = glu[:, :D_MIX] * _sigmoid(glu[:, D_MIX:])

    qkv = _dot(h, w_ref[0, :, 4 * D_MIX:7 * D_MIX])
    scale = LOG2E / math.sqrt(HEAD_DIM)
    for p in range(3 * N_PAIRS):
        blk = qkv[:, p * LANES:(p + 1) * LANES]
        qkv_ref[0, p] = (blk * scale if p < N_PAIRS else blk).astype(BF16)
    f_raw = _dot(h, w_ref[0, :, 7 * D_MIX:7 * D_MIX + LANES]) + bf_ref[...]

    n_chunks = tm // CHUNK
    lane = lax.broadcasted_iota(jnp.int32, (CHUNK, LANES), 1)
    w_row = lax.broadcasted_iota(jnp.int32, (CHUNK, 2 * CHUNK), 0)
    w_col = lax.broadcasted_iota(jnp.int32, (CHUNK, 2 * CHUNK), 1)
    causal_w = (w_col % CHUNK) <= w_row
    zero_bf = jnp.zeros((CHUNK, LANES), BF16)
    for p in range(N_PAIRS):
        cols = slice(p * LANES, (p + 1) * LANES)
        w_pair = jnp.where(causal_w, wsp_ref[p], jnp.zeros_like(wsp_ref[p]))
        rhs = []
        for c in range(n_chunks):
            vc = v_a[c * CHUNK:(c + 1) * CHUNK, cols]
            rhs.append(jnp.concatenate(
                [jnp.where(lane < GROUP_DIM, vc, zero_bf),
                 jnp.where(lane >= GROUP_DIM, vc, zero_bf)], axis=0))
        sv = _dot(w_pair, jnp.concatenate(rhs, axis=1))
        bias = bsp_ref[:, cols]
        for c in range(n_chunks):
            rows = slice(c * CHUNK, (c + 1) * CHUNK)
            ya_ref[0, rows, cols] = (
                u_a[rows, cols] * (sv[:, c * LANES:(c + 1) * LANES] + bias)).astype(BF16)

    first = CONV_HALO - (CONV_WIDTH - 1)
    n_rows = tm + CONV_HALO - SUBLANES
    conv = []
    for c in range(N_PAIRS):
        cols = slice(c * LANES, (c + 1) * LANES)
        for r in range(1, SUBLANES):
            sh_ref[c, r - 1] = zs_ref[r:r + n_rows, cols]
        acc = jnp.broadcast_to(cb_ref[:, cols], (tm, LANES))
        for j in range(CONV_WIDTH):
            r = (first + j) % SUBLANES
            base = first + j - r
            src = (zs_ref[base:base + tm, cols] if r == 0
                   else sh_ref[c, r - 1, base:base + tm, :])
            acc = acc + cw_ref[j:j + 1, cols] * src
        conv.append(acc)
    zs_ref[0:CONV_HALO, :] = zs_ref[tm:tm + CONV_HALO, :]

    log_f = jnp.minimum(f_raw, 0.0) - jnp.log1p(jnp.exp(-jnp.abs(f_raw)))
    t_row = lax.broadcasted_iota(jnp.int32, (CUM_CHUNK, CUM_CHUNK), 0)
    t_col = lax.broadcasted_iota(jnp.int32, (CUM_CHUNK, CUM_CHUNK), 1)
    tri = jnp.where(t_col <= t_row, 1.0, 0.0).astype(BF16)
    carry = carry_ref[...]
    for c in range(tm // CUM_CHUNK):
        rows = slice(c * CUM_CHUNK, (c + 1) * CUM_CHUNK)
        hi, mid, lo = _split3(log_f[rows])
        cs = _dot(tri, hi) + _dot(tri, mid) + _dot(tri, lo) + carry
        carry = cs[CUM_CHUNK - 1:CUM_CHUNK, :]
        frow_ref[0, :, rows] = cs.T[0:N_HEADS, :]
    carry_ref[...] = carry

    zn = _layer_norm(jnp.concatenate(conv, axis=1), clg_ref[...], clb_ref[...])
    yb_ref[0] = _dot((zn * _sigmoid(zn)).astype(BF16), wb_ref[0])


def _mixer_in_call(x, mod, g, w_in, wb, lng, lnb, wsp, bsp, cw, cb, clg, clb, bf, layer):
    b, s, d = x.shape
    tm = ROW_TILE
    n_cols = 7 * D_MIX + LANES
    const2 = lambda bi, si: (0, 0)
    const3 = lambda bi, si: (0, 0, 0)
    row3 = lambda bi, si: (bi, si, 0)
    out_shapes = (
        jax.ShapeDtypeStruct((b, s, D_MIX), BF16),
        jax.ShapeDtypeStruct((b, s, d), F32),
        jax.ShapeDtypeStruct((b, 3 * N_PAIRS, s, LANES), BF16),
        jax.ShapeDtypeStruct((b, N_HEADS, s), F32),
    )
    return pl.pallas_call(
        _mixer_in_kernel,
        grid=(b, s // tm),
        in_specs=[
            pl.BlockSpec((1, tm, d), row3),
            pl.BlockSpec((1, 1, N_MOD, d), lambda bi, si: (layer, bi, 0, 0)),
            pl.BlockSpec((1, d), const2),
            pl.BlockSpec((1, d, n_cols), lambda bi, si: (layer, 0, 0)),
            pl.BlockSpec((1, D_MIX, d), lambda bi, si: (layer, 0, 0)),
            pl.BlockSpec((1, D_MIX), const2),
            pl.BlockSpec((1, D_MIX), const2),
            pl.BlockSpec((N_PAIRS, CHUNK, 2 * CHUNK), const3),
            pl.BlockSpec((CHUNK, D_MIX), const2),
            pl.BlockSpec((CONV_HALO, D_MIX), const2),
            pl.BlockSpec((1, D_MIX), const2),
            pl.BlockSpec((1, D_MIX), const2),
            pl.BlockSpec((1, D_MIX), const2),
            pl.BlockSpec((1, LANES), const2),
        ],
        out_specs=(
            pl.BlockSpec((1, tm, D_MIX), row3),
            pl.BlockSpec((1, tm, d), row3),
            pl.BlockSpec((1, 3 * N_PAIRS, tm, LANES), lambda bi, si: (bi, 0, si, 0)),
            pl.BlockSpec((1, N_HEADS, tm), lambda bi, si: (bi, 0, si)),
        ),
        out_shape=out_shapes,
        scratch_shapes=[
            pltpu.VMEM((CONV_HALO + tm, D_MIX), F32),
            pltpu.VMEM((N_PAIRS, SUBLANES - 1, tm + CONV_HALO - SUBLANES, LANES), F32),
            pltpu.VMEM((1, LANES), F32),
        ],
        compiler_params=pltpu.CompilerParams(
            dimension_semantics=("arbitrary", "arbitrary"), vmem_limit_bytes=VMEM_LIMIT),
        name="mixer_in",
    )(x, mod, g, w_in, wb, lng, lnb, wsp, bsp, cw, cb, clg, clb, bf)


def _aug_key_rows(f):
    n = f.shape[1]
    hi, mid, lo = _split3(f)
    sub = lax.broadcasted_iota(jnp.int32, (N_HEADS, n), 0)
    ones3 = jnp.where(sub < 3, 1.0, 0.0)
    return jnp.concatenate(
        [-hi.astype(F32), -mid.astype(F32), -lo.astype(F32), jnp.zeros((N_HEADS, n), F32),
         ones3, jnp.zeros((LANES - 5 * N_HEADS, n), F32)], axis=0)


def _aug_query_rows(f, head, row):
    hi, mid, lo = _split3(f)
    ones = (row == head) | (row == head + N_HEADS) | (row == head + 2 * N_HEADS)
    base = jnp.where(ones, 1.0, 0.0)
    return jnp.where(row == 4 * N_HEADS, hi.astype(F32),
                     jnp.where(row == 4 * N_HEADS + 1, mid.astype(F32),
                               jnp.where(row == 4 * N_HEADS + 2, lo.astype(F32), base)))


def _attn_kernel(q_ref, k_ref, v_ref, frow_ref, o_ref,
                 kk_ref, vta_ref, vtb_ref, qq_ref, sta_ref, stb_ref, mba_ref, mbb_ref,
                 m_ref, acca_ref, accb_ref):
    s_len = k_ref.shape[2]
    tq = tk = ATTN_TILE
    pair = pl.program_id(1)

    ones_rows = jnp.ones((V_ROWS - HEAD_DIM, STAGE_CHUNK), BF16)
    for c in range(s_len // STAGE_CHUNK):
        cols = slice(c * STAGE_CHUNK, (c + 1) * STAGE_CHUNK)
        kk_ref[cols, 0:LANES] = k_ref[0, 0, cols, :]
        aug_t = _aug_key_rows(frow_ref[0, :, cols] * LOG2E)
        kk_ref[cols, LANES:2 * LANES] = aug_t.T.astype(BF16)
        vt = v_ref[0, 0, cols, :].astype(F32).T
        for vt_ref, head_rows in ((vta_ref, vt[:HEAD_DIM]), (vtb_ref, vt[HEAD_DIM:])):
            vt_ref[0:HEAD_DIM, cols] = head_rows.astype(BF16)
            vt_ref[HEAD_DIM:V_ROWS, cols] = ones_rows

    row = lax.broadcasted_iota(jnp.int32, (LANES, tq), 0)
    lo_rows = row < HEAD_DIM

    def produce(j, buf, masked):
        st_ref, mb_ref = buf
        start = pl.multiple_of(j * tk, tk)
        st = _dot(kk_ref[pl.ds(start, tk), :], qq_ref[...])
        if masked:
            r = lax.broadcasted_iota(jnp.int32, (tk, 2 * tq), 0)
            c = lax.broadcasted_iota(jnp.int32, (tk, 2 * tq), 1)
            st = jnp.where(r <= jnp.where(c >= tq, c - tq, c), st, -jnp.inf)
        st_ref[...] = st
        mb_ref[...] = jnp.max(st, axis=0, keepdims=True)

    def consume(j, buf):
        st_ref, mb_ref = buf
        start = pl.multiple_of(j * tk, tk)
        m_old = m_ref[...]
        m_new = jnp.maximum(m_old, mb_ref[...])
        alpha = jnp.exp2(m_old - m_new)
        m_ref[...] = m_new
        pt = jnp.exp2(st_ref[...] - m_new).astype(BF16)
        acca_ref[...] = (alpha[:, :tq] * acca_ref[...]
                         + _dot(vta_ref[:, pl.ds(start, tk)], pt[:, :tq]))
        accb_ref[...] = (alpha[:, tq:] * accb_ref[...]
                         + _dot(vtb_ref[:, pl.ds(start, tk)], pt[:, tq:]))

    buf_a = (sta_ref, mba_ref)
    buf_b = (stb_ref, mbb_ref)

    def q_tile(qi, carry):
        q_cols = pl.ds(pl.multiple_of(qi * tq, tq), tq)
        qt = q_ref[0, 0, q_cols, :].astype(F32).T
        top = jnp.concatenate([jnp.where(lo_rows, qt, 0.0), jnp.where(lo_rows, 0.0, qt)], axis=1)
        aug = []
        for hh in range(2):
            head = 2 * pair + hh
            f_q = frow_ref[0, pl.ds(head, 1), q_cols] * LOG2E
            aug.append(_aug_query_rows(f_q, head, row))
        qq_ref[...] = jnp.concatenate(
            [top, jnp.concatenate(aug, axis=1)], axis=0).astype(BF16)

        m_ref[...] = jnp.full(m_ref.shape, -1e30, F32)
        acca_ref[...] = jnp.zeros_like(acca_ref)
        accb_ref[...] = jnp.zeros_like(accb_ref)

        @pl.when(qi == 0)
        def _only_diagonal():
            produce(0, buf_a, masked=True)
            consume(0, buf_a)

        @pl.when(qi > 0)
        def _general():
            produce(0, buf_a, masked=False)
            n_loop = (qi - 1) // 2

            def body(i, c):
                j = 2 * i
                produce(j + 1, buf_b, masked=False)
                consume(j, buf_a)
                produce(j + 2, buf_a, masked=False)
                consume(j + 1, buf_b)
                return c

            lax.fori_loop(0, n_loop, body, 0)
            done = 2 * n_loop

            @pl.when(qi - done == 1)
            def _tail_two():
                produce(qi, buf_b, masked=True)
                consume(done, buf_a)
                consume(qi, buf_b)

            @pl.when(qi - done == 2)
            def _tail_three():
                produce(done + 1, buf_b, masked=False)
                consume(done, buf_a)
                produce(qi, buf_a, masked=True)
                consume(done + 1, buf_b)
                consume(qi, buf_a)

        ot = jnp.concatenate(
            [acc_ref[0:HEAD_DIM, :] * (1.0 / acc_ref[HEAD_DIM:HEAD_DIM + 1, :])
             for acc_ref in (acca_ref, accb_ref)], axis=0)
        o_ref[0, 0, q_cols, :] = ot.T.astype(BF16)
        return carry

    lax.fori_loop(0, s_len // tq, q_tile, 0)


def _attn_call(qkv, frow):
    b, _, s, _ = qkv.shape
    tq = ATTN_TILE
    return pl.pallas_call(
        _attn_kernel,
        grid=(b, N_PAIRS),
        in_specs=[
            pl.BlockSpec((1, 1, s, LANES), lambda bi, p: (bi, p, 0, 0)),
            pl.BlockSpec((1, 1, s, LANES), lambda bi, p: (bi, N_PAIRS + p, 0, 0)),
            pl.BlockSpec((1, 1, s, LANES), lambda bi, p: (bi, 2 * N_PAIRS + p, 0, 0)),
            pl.BlockSpec((1, N_HEADS, s), lambda bi, p: (bi, 0, 0)),
        ],
        out_specs=pl.BlockSpec((1, 1, s, LANES), lambda bi, p: (bi, p, 0, 0)),
        out_shape=jax.ShapeDtypeStruct((b, N_PAIRS, s, LANES), BF16),
        scratch_shapes=[
            pltpu.VMEM((s, 2 * LANES), BF16),
            pltpu.VMEM((V_ROWS, s), BF16),
            pltpu.VMEM((V_ROWS, s), BF16),
            pltpu.VMEM((2 * LANES, 2 * tq), BF16),
            pltpu.VMEM((tq, 2 * tq), F32),
            pltpu.VMEM((tq, 2 * tq), F32),
            pltpu.VMEM((1, 2 * tq), F32),
            pltpu.VMEM((1, 2 * tq), F32),
            pltpu.VMEM((1, 2 * tq), F32),
            pltpu.VMEM((V_ROWS, tq), F32),
            pltpu.VMEM((V_ROWS, tq), F32),
        ],
        compiler_params=pltpu.CompilerParams(
            dimension_semantics=("arbitrary", "arbitrary"), vmem_limit_bytes=VMEM_LIMIT),
        name="fox_attn",
    )(qkv, qkv, qkv, frow)


def _merge_kernel(x_ref, mod_ref, gpre_ref, gpost_ref, ya_ref, yb_ref, yc_ref,
                  wgate_ref, wa_ref, wc_ref, wout_ref, o_ref):
    x = x_ref[0]
    mod = mod_ref[0, 0]
    sh1 = mod[0:1]
    sc1 = mod[1:2]
    gt1 = mod[2:3]
    h = (_rms(x) * (gpre_ref[...] * (1.0 + sc1)) + sh1).astype(BF16)

    def gate(br):
        return _sigmoid(_dot(h, wgate_ref[0, :, br * D_MODEL:(br + 1) * D_MODEL]))

    merged = (gate(0) * _dot(ya_ref[0], wa_ref[0]) + gate(1) * yb_ref[0]
              + gate(2) * _dot(jnp.concatenate([yc_ref[0, p] for p in range(N_PAIRS)], axis=1),
                               wc_ref[0]))
    y = _dot(merged.astype(BF16), wout_ref[0])
    o_ref[0] = x + gt1 * (_rms(y) * gpost_ref[...])


def _merge_call(x, mod, gpre, gpost, ya, yb, yc, wgate, wa, wc, wout, layer):
    b, s, d = x.shape
    tm = ROW_TILE
    const2 = lambda bi, si: (0, 0)
    lay3 = lambda bi, si: (layer, 0, 0)
    row3 = lambda bi, si: (bi, si, 0)
    return pl.pallas_call(
        _merge_kernel,
        grid=(b, s // tm),
        in_specs=[
            pl.BlockSpec((1, tm, d), row3),
            pl.BlockSpec((1, 1, N_MOD, d), lambda bi, si: (layer, bi, 0, 0)),
            pl.BlockSpec((1, d), const2),
            pl.BlockSpec((1, d), const2),
            pl.BlockSpec((1, tm, D_MIX), row3),
            pl.BlockSpec((1, tm, d), row3),
            pl.BlockSpec((1, N_PAIRS, tm, LANES), lambda bi, si: (bi, 0, si, 0)),
            pl.BlockSpec((1, d, N_BRANCH * d), lay3),
            pl.BlockSpec((1, D_MIX, d), lay3),
            pl.BlockSpec((1, D_MIX, d), lay3),
            pl.BlockSpec((1, d, d), lay3),
        ],
        out_specs=pl.BlockSpec((1, tm, d), row3),
        out_shape=jax.ShapeDtypeStruct((b, s, d), F32),
        compiler_params=pltpu.CompilerParams(
            dimension_semantics=("arbitrary", "arbitrary"), vmem_limit_bytes=VMEM_LIMIT),
        name="mixer_merge",
    )(x, mod, gpre, gpost, ya, yb, yc, wgate, wa, wc, wout)


def _mlp_kernel(x_ref, mod_ref, gpre_ref, gpost_ref, w1_ref, w2_ref, o_ref):
    x = x_ref[0]
    mod = mod_ref[0, 0]
    sh2 = mod[3:4]
    sc2 = mod[4:5]
    gt2 = mod[5:6]
    h = (_rms(x) * (gpre_ref[...] * (1.0 + sc2)) + sh2).astype(BF16)
    y = None
    for c in range(D_FF // D_MODEL):
        cols = slice(c * D_MODEL, (c + 1) * D_MODEL)
        a = jnp.maximum(_dot(h, w1_ref[0, :, cols]), 0.0)
        t = _dot((a * a).astype(BF16), w2_ref[0, cols, :])
        y = t if y is None else y + t
    o_ref[0] = x + gt2 * (_rms(y) * gpost_ref[...])


def _mlp_call(x, mod, gpre, gpost, w1, w2, layer):
    b, s, d = x.shape
    tm = ROW_TILE
    const2 = lambda bi, si: (0, 0)
    lay3 = lambda bi, si: (layer, 0, 0)
    row3 = lambda bi, si: (bi, si, 0)
    return pl.pallas_call(
        _mlp_kernel,
        grid=(b, s // tm),
        in_specs=[
            pl.BlockSpec((1, tm, d), row3),
            pl.BlockSpec((1, 1, N_MOD, d), lambda bi, si: (layer, bi, 0, 0)),
            pl.BlockSpec((1, d), const2),
            pl.BlockSpec((1, d), const2),
            pl.BlockSpec((1, d, D_FF), lay3),
            pl.BlockSpec((1, D_FF, d), lay3),
        ],
        out_specs=pl.BlockSpec((1, tm, d), row3),
        out_shape=jax.ShapeDtypeStruct((b, s, d), F32),
        compiler_params=pltpu.CompilerParams(
            dimension_semantics=("arbitrary", "arbitrary"), vmem_limit_bytes=VMEM_LIMIT),
        name="relu2_mlp",
    )(x, mod, gpre, gpost, w1, w2)


def kernel(x, c, ada_w, ada_b, mix_pre_g, mix_post_g, mlp_pre_g, mlp_post_g, w_in, gmlp_ln_g, gmlp_ln_b, gmlp_ws, gmlp_bs, w_a_out, conv_w, conv_b, conv_ln_g, conv_ln_b, w_b_out, fox_bf, w_c_out, w_out, mlp_w1, mlp_w2):
    n_layers = ada_w.shape[0]
    b, s, d = x.shape
    assert d == D_MODEL and s % ROW_TILE == 0 and s % ATTN_TILE == 0
    assert ROW_TILE % CUM_CHUNK == 0 and ROW_TILE % CHUNK == 0

    mod = _ada_call(c, ada_w, ada_b).reshape(n_layers, b, N_MOD, d)

    o_gate = 7 * D_MIX + N_HEADS
    w_in_b = w_in[:, :, :7 * D_MIX + LANES].astype(BF16)
    w_gate_b = w_in[:, :, o_gate:].astype(BF16)
    w_a_b, w_b_b, w_c_b = w_a_out.astype(BF16), w_b_out.astype(BF16), w_c_out.astype(BF16)
    w_out_b, w1_b, w2_b = w_out.astype(BF16), mlp_w1.astype(BF16), mlp_w2.astype(BF16)

    row = lambda a: a.reshape(1, -1)
    for l in range(n_layers):
        bf = jnp.pad(fox_bf[l], (0, LANES - N_HEADS)).reshape(1, LANES)
        wsp = gmlp_ws[l].reshape(N_PAIRS, 2, CHUNK, CHUNK).transpose(0, 2, 1, 3)
        wsp = wsp.reshape(N_PAIRS, CHUNK, 2 * CHUNK).astype(BF16)
        bsp = jnp.repeat(gmlp_bs[l].T, GROUP_DIM, axis=1)
        cw = jnp.pad(conv_w[l], ((0, CONV_HALO - CONV_WIDTH), (0, 0)))

        ya, yb, qkv, frow = _mixer_in_call(
            x, mod, row(mix_pre_g[l]), w_in_b, w_b_b,
            row(gmlp_ln_g[l]), row(gmlp_ln_b[l]), wsp, bsp, cw, row(conv_b[l]),
            row(conv_ln_g[l]), row(conv_ln_b[l]), bf, l)
        yc = _attn_call(qkv, frow)
        x = _merge_call(
            x, mod, row(mix_pre_g[l]), row(mix_post_g[l]), ya, yb, yc, w_gate_b,
            w_a_b, w_c_b, w_out_b, l)
        x = _mlp_call(x, mod, row(mlp_pre_g[l]), row(mlp_post_g[l]), w1_b, w2_b, l)
    return x
```

```python
import math

import jax
import jax.numpy as jnp
from jax import lax
from jax.experimental import pallas as pl
from jax.experimental.pallas import tpu as pltpu

D_MODEL = 1024
D_MIX = 512
N_GROUPS = 8
CHUNK = 128
GROUP_DIM = D_MIX // N_GROUPS
CONV_WIDTH = 31
N_HEADS = 8
HEAD_DIM = D_MIX // N_HEADS
N_BRANCH = 3
D_FF = 4 * D_MODEL
N_MOD = 6
NORM_EPS = 1e-6

LANES = 128
SUBLANES = 8
N_PAIRS = D_MIX // LANES
CONV_HALO = 32
CUM_CHUNK = 256
VMEM_LIMIT = 56 * 1024 * 1024

ROW_TILE = 512
ATTN_TILE = 1024
STAGE_CHUNK = 512
V_ROWS = HEAD_DIM + 16
LOG2E = math.log2(math.e)

F32 = jnp.float32
BF16 = jnp.bfloat16


def _dot(a, b):
    return jnp.dot(a, b, preferred_element_type=F32)


def _sigmoid(x):
    return 0.5 * jnp.tanh(0.5 * x) + 0.5


def _gelu_tanh(x):
    c = math.sqrt(2.0 / math.pi)
    return 0.5 * x * (1.0 + jnp.tanh(c * (x + 0.044715 * (x * x * x))))


def _rms(x):
    return x * lax.rsqrt(jnp.mean(x * x, axis=-1, keepdims=True) + NORM_EPS)


def _layer_norm(x, g, b):
    mu = jnp.mean(x, axis=-1, keepdims=True)
    xc = x - mu
    var = jnp.mean(xc * xc, axis=-1, keepdims=True)
    return xc * lax.rsqrt(var + NORM_EPS) * g + b


def _split3(x):
    hi = x.astype(BF16)
    r = x - hi.astype(F32)
    mid = r.astype(BF16)
    lo = (r - mid.astype(F32)).astype(BF16)
    return hi, mid, lo


def _ada_kernel(c_ref, w_ref, b_ref, o_ref):
    c = c_ref[...]
    c_act = (c * _sigmoid(c)).astype(BF16)
    o_ref[0] = _dot(c_act, w_ref[0].astype(BF16)) + b_ref[0]


def _ada_call(c, ada_w, ada_b):
    n_layers, d, n_out = ada_w.shape
    b = c.shape[0]
    tn = 2 * D_MODEL
    return pl.pallas_call(
        _ada_kernel,
        grid=(n_layers, n_out // tn),
        in_specs=[
            pl.BlockSpec((b, d), lambda l, j: (0, 0)),
            pl.BlockSpec((1, d, tn), lambda l, j: (l, 0, j)),
            pl.BlockSpec((1, 1, tn), lambda l, j: (l, 0, j)),
        ],
        out_specs=pl.BlockSpec((1, b, tn), lambda l, j: (l, 0, j)),
        out_shape=jax.ShapeDtypeStruct((n_layers, b, n_out), F32),
        compiler_params=pltpu.CompilerParams(
            dimension_semantics=("arbitrary", "arbitrary"), vmem_limit_bytes=VMEM_LIMIT),
        name="ada_mod",
    )(c, ada_w, ada_b.reshape(n_layers, 1, n_out))


def _mixer_in_kernel(x_ref, mod_ref, g_ref, w_ref, wb_ref,
                     lng_ref, lnb_ref, wsp_ref, bsp_ref, cw_ref, cb_ref, clg_ref, clb_ref,
                     bf_ref,
                     ya_ref, yb_ref, qkv_ref, frow_ref,
                     zs_ref, sh_ref, carry_ref):
    tm = x_ref.shape[1]
    s_idx = pl.program_id(1)

    @pl.when(s_idx == 0)
    def _():
        zs_ref[0:CONV_HALO, :] = jnp.zeros((CONV_HALO, D_MIX), F32)
        carry_ref[...] = jnp.zeros_like(carry_ref)

    x = x_ref[0]
    mod = mod_ref[0, 0]
    sh1 = mod[0:1]
    sc1 = mod[1:2]
    h = (_rms(x) * (g_ref[...] * (1.0 + sc1)) + sh1).astype(BF16)

    uv_raw = _dot(h, w_ref[0, :, 0:2 * D_MIX])
    glu = _dot(h, w_ref[0, :, 2 * D_MIX:4 * D_MIX])

    uv = _gelu_tanh(uv_raw)
    u_a = uv[:, :D_MIX]
    v_a = _layer_norm(uv[:, D_MIX:], lng_ref[...], lnb_ref[...]).astype(BF16)

    zs_ref[CONV_HALO:CONV_HALO + tm, :] = glu[:, :D_MIX] * _sigmoid(glu[:, D_MIX:])

    qkv = _dot(h, w_ref[0, :, 4 * D_MIX:7 * D_MIX])
    scale = LOG2E / math.sqrt(HEAD_DIM)
    for p in range(3 * N_PAIRS):
        blk = qkv[:, p * LANES:(p + 1) * LANES]
        qkv_ref[0, p] = (blk * scale if p < N_PAIRS else blk).astype(BF16)
    f_raw = _dot(h, w_ref[0, :, 7 * D_MIX:7 * D_MIX + LANES]) + bf_ref[...]

    n_chunks = tm // CHUNK
    lane = lax.broadcasted_iota(jnp.int32, (CHUNK, LANES), 1)
    w_row = lax.broadcasted_iota(jnp.int32, (CHUNK, 2 * CHUNK), 0)
    w_col = lax.broadcasted_iota(jnp.int32, (CHUNK, 2 * CHUNK), 1)
    causal_w = (w_col % CHUNK) <= w_row
    zero_bf = jnp.zeros((CHUNK, LANES), BF16)
    for p in range(N_PAIRS):
        cols = slice(p * LANES, (p + 1) * LANES)
        w_pair = jnp.where(causal_w, wsp_ref[p], jnp.zeros_like(wsp_ref[p]))
        rhs = []
        for c in range(n_chunks):
            vc = v_a[c * CHUNK:(c + 1) * CHUNK, cols]
            rhs.append(jnp.concatenate(
                [jnp.where(lane < GROUP_DIM, vc, zero_bf),
                 jnp.where(lane >= GROUP_DIM, vc, zero_bf)], axis=0))
        sv = _dot(w_pair, jnp.concatenate(rhs, axis=1))
        bias = bsp_ref[:, cols]
        for c in range(n_chunks):
            rows = slice(c * CHUNK, (c + 1) * CHUNK)
            ya_ref[0, rows, cols] = (
                u_a[rows, cols] * (sv[:, c * LANES:(c + 1) * LANES] + bias)).astype(BF16)

    first = CONV_HALO - (CONV_WIDTH - 1)
    n_rows = tm + CONV_HALO - SUBLANES
    conv = []
    for c in range(N_PAIRS):
        cols = slice(c * LANES, (c + 1) * LANES)
        for r in range(1, SUBLANES):
            sh_ref[c, r - 1] = zs_ref[r:r + n_rows, cols]
        acc = jnp.broadcast_to(cb_ref[:, cols], (tm, LANES))
        for j in range(CONV_WIDTH):
            r = (first + j) % SUBLANES
            base = first + j - r
            src = (zs_ref[base:base + tm, cols] if r == 0
                   else sh_ref[c, r - 1, base:base + tm, :])
            acc = acc + cw_ref[j:j + 1, cols] * src
        conv.append(acc)
    zs_ref[0:CONV_HALO, :] = zs_ref[tm:tm + CONV_HALO, :]

    log_f = jnp.minimum(f_raw, 0.0) - jnp.log1p(jnp.exp(-jnp.abs(f_raw)))
    t_row = lax.broadcasted_iota(jnp.int32, (CUM_CHUNK, CUM_CHUNK), 0)
    t_col = lax.broadcasted_iota(jnp.int32, (CUM_CHUNK, CUM_CHUNK), 1)
    tri = jnp.where(t_col <= t_row, 1.0, 0.0).astype(BF16)
    carry = carry_ref[...]
    for c in range(tm // CUM_CHUNK):
        rows = slice(c * CUM_CHUNK, (c + 1) * CUM_CHUNK)
        hi, mid, lo = _split3(log_f[rows])
        cs = _dot(tri, hi) + _dot(tri, mid) + _dot(tri, lo) + carry
        carry = cs[CUM_CHUNK - 1:CUM_CHUNK, :]
        frow_ref[0, :, rows] = cs.T[0:N_HEADS, :]
    carry_ref[...] = carry

    zn = _layer_norm(jnp.concatenate(conv, axis=1), clg_ref[...], clb_ref[...])
    yb_ref[0] = _dot((zn * _sigmoid(zn)).astype(BF16), wb_ref[0])


def _mixer_in_call(x, mod, g, w_in, wb, lng, lnb, wsp, bsp, cw, cb, clg, clb, bf, layer):
    b, s, d = x.shape
    tm = ROW_TILE
    n_cols = 7 * D_MIX + LANES
    const2 = lambda bi, si: (0, 0)
    const3 = lambda bi, si: (0, 0, 0)
    row3 = lambda bi, si: (bi, si, 0)
    out_shapes = (
        jax.ShapeDtypeStruct((b, s, D_MIX), BF16),
        jax.ShapeDtypeStruct((b, s, d), F32),
        jax.ShapeDtypeStruct((b, 3 * N_PAIRS, s, LANES), BF16),
        jax.ShapeDtypeStruct((b, N_HEADS, s), F32),
    )
    return pl.pallas_call(
        _mixer_in_kernel,
        grid=(b, s // tm),
        in_specs=[
            pl.BlockSpec((1, tm, d), row3),
            pl.BlockSpec((1, 1, N_MOD, d), lambda bi, si: (layer, bi, 0, 0)),
            pl.BlockSpec((1, d), const2),
            pl.BlockSpec((1, d, n_cols), lambda bi, si: (layer, 0, 0)),
            pl.BlockSpec((1, D_MIX, d), lambda bi, si: (layer, 0, 0)),
            pl.BlockSpec((1, D_MIX), const2),
            pl.BlockSpec((1, D_MIX), const2),
            pl.BlockSpec((N_PAIRS, CHUNK, 2 * CHUNK), const3),
            pl.BlockSpec((CHUNK, D_MIX), const2),
            pl.BlockSpec((CONV_HALO, D_MIX), const2),
            pl.BlockSpec((1, D_MIX), const2),
            pl.BlockSpec((1, D_MIX), const2),
            pl.BlockSpec((1, D_MIX), const2),
            pl.BlockSpec((1, LANES), const2),
        ],
        out_specs=(
            pl.BlockSpec((1, tm, D_MIX), row3),
            pl.BlockSpec((1, tm, d), row3),
            pl.BlockSpec((1, 3 * N_PAIRS, tm, LANES), lambda bi, si: (bi, 0, si, 0)),
            pl.BlockSpec((1, N_HEADS, tm), lambda bi, si: (bi, 0, si)),
        ),
        out_shape=out_shapes,
        scratch_shapes=[
            pltpu.VMEM((CONV_HALO + tm, D_MIX), F32),
            pltpu.VMEM((N_PAIRS, SUBLANES - 1, tm + CONV_HALO - SUBLANES, LANES), F32),
            pltpu.VMEM((1, LANES), F32),
        ],
        compiler_params=pltpu.CompilerParams(
            dimension_semantics=("arbitrary", "arbitrary"), vmem_limit_bytes=VMEM_LIMIT),
        name="mixer_in",
    )(x, mod, g, w_in, wb, lng, lnb, wsp, bsp, cw, cb, clg, clb, bf)


def _aug_key_rows(f):
    n = f.shape[1]
    hi, mid, lo = _split3(f)
    sub = lax.broadcasted_iota(jnp.int32, (N_HEADS, n), 0)
    ones3 = jnp.where(sub < 3, 1.0, 0.0)
    return jnp.concatenate(
        [-hi.astype(F32), -mid.astype(F32), -lo.astype(F32), jnp.zeros((N_HEADS, n), F32),
         ones3, jnp.zeros((LANES - 5 * N_HEADS, n), F32)], axis=0)


def _aug_query_rows(f, head, row):
    hi, mid, lo = _split3(f)
    ones = (row == head) | (row == head + N_HEADS) | (row == head + 2 * N_HEADS)
    base = jnp.where(ones, 1.0, 0.0)
    return jnp.where(row == 4 * N_HEADS, hi.astype(F32),
                     jnp.where(row == 4 * N_HEADS + 1, mid.astype(F32),
                               jnp.where(row == 4 * N_HEADS + 2, lo.astype(F32), base)))


def _attn_kernel(q_ref, k_ref, v_ref, frow_ref, o_ref,
                 kk_ref, vta_ref, vtb_ref, qq_ref, sta_ref, stb_ref, mba_ref, mbb_ref,
                 m_ref, acca_ref, accb_ref):
    s_len = k_ref.shape[2]
    tq = tk = ATTN_TILE
    pair = pl.program_id(1)

    ones_rows = jnp.ones((V_ROWS - HEAD_DIM, STAGE_CHUNK), BF16)
    for c in range(s_len // STAGE_CHUNK):
        cols = slice(c * STAGE_CHUNK, (c + 1) * STAGE_CHUNK)
        kk_ref[cols, 0:LANES] = k_ref[0, 0, cols, :]
        aug_t = _aug_key_rows(frow_ref[0, :, cols] * LOG2E)
        kk_ref[cols, LANES:2 * LANES] = aug_t.T.astype(BF16)
        vt = v_ref[0, 0, cols, :].astype(F32).T
        for vt_ref, head_rows in ((vta_ref, vt[:HEAD_DIM]), (vtb_ref, vt[HEAD_DIM:])):
            vt_ref[0:HEAD_DIM, cols] = head_rows.astype(BF16)
            vt_ref[HEAD_DIM:V_ROWS, cols] = ones_rows

    row = lax.broadcasted_iota(jnp.int32, (LANES, tq), 0)
    lo_rows = row < HEAD_DIM

    half = tk // 2
    acc_refs = (acca_ref, accb_ref)
    vt_refs = (vta_ref, vtb_ref)

    def produce(j, buf):
        st_ref, mb_ref = buf
        start = pl.multiple_of(j * tk, tk)
        st = _dot(kk_ref[pl.ds(start, tk), :], qq_ref[...])
        st_ref[...] = st
        mb_ref[...] = jnp.max(st, axis=0, keepdims=True)

    def produce_diag(j, buf):
        st_ref, mb_ref = buf
        start = pl.multiple_of(j * tk, tk)
        r = lax.broadcasted_iota(jnp.int32, (half, 2 * tq), 0)
        c = lax.broadcasted_iota(jnp.int32, (half, 2 * tq), 1)
        st = _dot(kk_ref[pl.ds(start, half), :], qq_ref[...])
        st = jnp.where(r <= jnp.where(c >= tq, c - tq, c), st, -jnp.inf)
        st_ref[0:half, :] = st
        mb_ref[...] = jnp.max(st, axis=0, keepdims=True)
        qq_late = jnp.concatenate([qq_ref[:, half:tq], qq_ref[:, tq + half:2 * tq]], axis=1)
        r = lax.broadcasted_iota(jnp.int32, (half, tq), 0)
        c = lax.broadcasted_iota(jnp.int32, (half, tq), 1)
        st = _dot(kk_ref[pl.ds(pl.multiple_of(start + half, half), half), :], qq_late)
        st_ref[half:tk, 0:tq] = jnp.where(r <= jnp.where(c >= half, c - half, c), st, -jnp.inf)

    def consume(j, buf):
        st_ref, mb_ref = buf
        start = pl.multiple_of(j * tk, tk)
        m_old = m_ref[...]
        m_new = jnp.maximum(m_old, mb_ref[...])
        alpha = jnp.exp2(m_old - m_new)
        m_ref[...] = m_new
        pt = jnp.exp2(st_ref[...] - m_new).astype(BF16)
        for hh in range(2):
            cols = slice(hh * tq, (hh + 1) * tq)
            acc_refs[hh][...] = (alpha[:, cols] * acc_refs[hh][...]
                                 + _dot(vt_refs[hh][:, pl.ds(start, tk)], pt[:, cols]))

    def consume_diag(j, buf):
        st_ref, mb_ref = buf
        start = pl.multiple_of(j * tk, tk)
        m_old = m_ref[...]
        m_new = jnp.maximum(m_old, mb_ref[...])
        alpha = jnp.exp2(m_old - m_new)
        m_ref[...] = m_new
        pt = jnp.exp2(st_ref[0:half, :] - m_new).astype(BF16)
        for hh in range(2):
            cols = slice(hh * tq, (hh + 1) * tq)
            acc_refs[hh][...] = (alpha[:, cols] * acc_refs[hh][...]
                                 + _dot(vt_refs[hh][:, pl.ds(start, half)], pt[:, cols]))
        start2 = pl.multiple_of(start + half, half)
        for hh in range(2):
            cols = slice(hh * tq + half, (hh + 1) * tq)
            st = st_ref[half:tk, hh * half:(hh + 1) * half]
            m_old = m_ref[:, cols]
            m_new = jnp.maximum(m_old, jnp.max(st, axis=0, keepdims=True))
            alpha = jnp.exp2(m_old - m_new)
            m_ref[:, cols] = m_new
            pt = jnp.exp2(st - m_new).astype(BF16)
            acc_refs[hh][:, half:tq] = (alpha * acc_refs[hh][:, half:tq]
                                        + _dot(vt_refs[hh][:, pl.ds(start2, half)], pt))

    buf_a = (sta_ref, mba_ref)
    buf_b = (stb_ref, mbb_ref)

    def q_tile(qi, carry):
        q_cols = pl.ds(pl.multiple_of(qi * tq, tq), tq)
        qt = q_ref[0, 0, q_cols, :].astype(F32).T
        top = jnp.concatenate([jnp.where(lo_rows, qt, 0.0), jnp.where(lo_rows, 0.0, qt)], axis=1)
        aug = []
        for hh in range(2):
            head = 2 * pair + hh
            f_q = frow_ref[0, pl.ds(head, 1), q_cols] * LOG2E
            aug.append(_aug_query_rows(f_q, head, row))
        qq_ref[...] = jnp.concatenate(
            [top, jnp.concatenate(aug, axis=1)], axis=0).astype(BF16)

        m_ref[...] = jnp.full(m_ref.shape, -1e30, F32)
        acca_ref[...] = jnp.zeros_like(acca_ref)
        accb_ref[...] = jnp.zeros_like(accb_ref)

        @pl.when(qi == 0)
        def _only_diagonal():
            produce_diag(0, buf_a)
            consume_diag(0, buf_a)

        @pl.when(qi > 0)
        def _general():
            produce(0, buf_a)
            n_loop = (qi - 1) // 2

            def body(i, c):
                j = 2 * i
                produce(j + 1, buf_b)
                consume(j, buf_a)
                produce(j + 2, buf_a)
                consume(j + 1, buf_b)
                return c

            lax.fori_loop(0, n_loop, body, 0)
            done = 2 * n_loop

            @pl.when(qi - done == 1)
            def _tail_two():
                produce_diag(qi, buf_b)
                consume(done, buf_a)
                consume_diag(qi, buf_b)

            @pl.when(qi - done == 2)
            def _tail_three():
                produce(done + 1, buf_b)
                consume(done, buf_a)
                produce_diag(qi, buf_a)
                consume(done + 1, buf_b)
                consume_diag(qi, buf_a)

        ot = jnp.concatenate(
            [acc_ref[0:HEAD_DIM, :] * (1.0 / acc_ref[HEAD_DIM:HEAD_DIM + 1, :])
             for acc_ref in (acca_ref, accb_ref)], axis=0)
        o_ref[0, 0, q_cols, :] = ot.T.astype(BF16)
        return carry

    lax.fori_loop(0, s_len // tq, q_tile, 0)


def _attn_call(qkv, frow):
    b, _, s, _ = qkv.shape
    tq = ATTN_TILE
    return pl.pallas_call(
        _attn_kernel,
        grid=(b, N_PAIRS),
        in_specs=[
            pl.BlockSpec((1, 1, s, LANES), lambda bi, p: (bi, p, 0, 0)),
            pl.BlockSpec((1, 1, s, LANES), lambda bi, p: (bi, N_PAIRS + p, 0, 0)),
            pl.BlockSpec((1, 1, s, LANES), lambda bi, p: (bi, 2 * N_PAIRS + p, 0, 0)),
            pl.BlockSpec((1, N_HEADS, s), lambda bi, p: (bi, 0, 0)),
        ],
        out_specs=pl.BlockSpec((1, 1, s, LANES), lambda bi, p: (bi, p, 0, 0)),
        out_shape=jax.ShapeDtypeStruct((b, N_PAIRS, s, LANES), BF16),
        scratch_shapes=[
            pltpu.VMEM((s, 2 * LANES), BF16),
            pltpu.VMEM((V_ROWS, s), BF16),
            pltpu.VMEM((V_ROWS, s), BF16),
            pltpu.VMEM((2 * LANES, 2 * tq), BF16),
            pltpu.VMEM((tq, 2 * tq), F32),
            pltpu.VMEM((tq, 2 * tq), F32),
            pltpu.VMEM((1, 2 * tq), F32),
            pltpu.VMEM((1, 2 * tq), F32),
            pltpu.VMEM((1, 2 * tq), F32),
            pltpu.VMEM((V_ROWS, tq), F32),
            pltpu.VMEM((V_ROWS, tq), F32),
        ],
        compiler_params=pltpu.CompilerParams(
            dimension_semantics=("arbitrary", "arbitrary"), vmem_limit_bytes=VMEM_LIMIT),
        name="fox_attn",
    )(qkv, qkv, qkv, frow)


def _merge_kernel(x_ref, mod_ref, gpre_ref, gpost_ref, ya_ref, yb_ref, yc_ref,
                  wgate_ref, wa_ref, wc_ref, wout_ref, o_ref):
    x = x_ref[0]
    mod = mod_ref[0, 0]
    sh1 = mod[0:1]
    sc1 = mod[1:2]
    gt1 = mod[2:3]
    h = (_rms(x) * (gpre_ref[...] * (1.0 + sc1)) + sh1).astype(BF16)

    def gate(br):
        return _sigmoid(_dot(h, wgate_ref[0, :, br * D_MODEL:(br + 1) * D_MODEL]))

    merged = (gate(0) * _dot(ya_ref[0], wa_ref[0]) + gate(1) * yb_ref[0]
              + gate(2) * _dot(jnp.concatenate([yc_ref[0, p] for p in range(N_PAIRS)], axis=1),
                               wc_ref[0]))
    y = _dot(merged.astype(BF16), wout_ref[0])
    o_ref[0] = x + gt1 * (_rms(y) * gpost_ref[...])


def _merge_call(x, mod, gpre, gpost, ya, yb, yc, wgate, wa, wc, wout, layer):
    b, s, d = x.shape
    tm = ROW_TILE
    const2 = lambda bi, si: (0, 0)
    lay3 = lambda bi, si: (layer, 0, 0)
    row3 = lambda bi, si: (bi, si, 0)
    return pl.pallas_call(
        _merge_kernel,
        grid=(b, s // tm),
        in_specs=[
            pl.BlockSpec((1, tm, d), row3),
            pl.BlockSpec((1, 1, N_MOD, d), lambda bi, si: (layer, bi, 0, 0)),
            pl.BlockSpec((1, d), const2),
            pl.BlockSpec((1, d), const2),
            pl.BlockSpec((1, tm, D_MIX), row3),
            pl.BlockSpec((1, tm, d), row3),
            pl.BlockSpec((1, N_PAIRS, tm, LANES), lambda bi, si: (bi, 0, si, 0)),
            pl.BlockSpec((1, d, N_BRANCH * d), lay3),
            pl.BlockSpec((1, D_MIX, d), lay3),
            pl.BlockSpec((1, D_MIX, d), lay3),
            pl.BlockSpec((1, d, d), lay3),
        ],
        out_specs=pl.BlockSpec((1, tm, d), row3),
        out_shape=jax.ShapeDtypeStruct((b, s, d), F32),
        compiler_params=pltpu.CompilerParams(
            dimension_semantics=("arbitrary", "arbitrary"), vmem_limit_bytes=VMEM_LIMIT),
        name="mixer_merge",
    )(x, mod, gpre, gpost, ya, yb, yc, wgate, wa, wc, wout)


def _mlp_kernel(x_ref, mod_ref, gpre_ref, gpost_ref, w1_ref, w2_ref, o_ref):
    x = x_ref[0]
    mod = mod_ref[0, 0]
    sh2 = mod[3:4]
    sc2 = mod[4:5]
    gt2 = mod[5:6]
    h = (_rms(x) * (gpre_ref[...] * (1.0 + sc2)) + sh2).astype(BF16)
    y = None
    for c in range(D_FF // D_MODEL):
        cols = slice(c * D_MODEL, (c + 1) * D_MODEL)
        a = jnp.maximum(_dot(h, w1_ref[0, :, cols]), 0.0)
        t = _dot((a * a).astype(BF16), w2_ref[0, cols, :])
        y = t if y is None else y + t
    o_ref[0] = x + gt2 * (_rms(y) * gpost_ref[...])


def _mlp_call(x, mod, gpre, gpost, w1, w2, layer):
    b, s, d = x.shape
    tm = ROW_TILE
    const2 = lambda bi, si: (0, 0)
    lay3 = lambda bi, si: (layer, 0, 0)
    row3 = lambda bi, si: (bi, si, 0)
    return pl.pallas_call(
        _mlp_kernel,
        grid=(b, s // tm),
        in_specs=[
            pl.BlockSpec((1, tm, d), row3),
            pl.BlockSpec((1, 1, N_MOD, d), lambda bi, si: (layer, bi, 0, 0)),
            pl.BlockSpec((1, d), const2),
            pl.BlockSpec((1, d), const2),
            pl.BlockSpec((1, d, D_FF), lay3),
            pl.BlockSpec((1, D_FF, d), lay3),
        ],
        out_specs=pl.BlockSpec((1, tm, d), row3),
        out_shape=jax.ShapeDtypeStruct((b, s, d), F32),
        compiler_params=pltpu.CompilerParams(
            dimension_semantics=("arbitrary", "arbitrary"), vmem_limit_bytes=VMEM_LIMIT),
        name="relu2_mlp",
    )(x, mod, gpre, gpost, w1, w2)


def kernel(x, c, ada_w, ada_b, mix_pre_g, mix_post_g, mlp_pre_g, mlp_post_g, w_in, gmlp_ln_g, gmlp_ln_b, gmlp_ws, gmlp_bs, w_a_out, conv_w, conv_b, conv_ln_g, conv_ln_b, w_b_out, fox_bf, w_c_out, w_out, mlp_w1, mlp_w2):
    n_layers = ada_w.shape[0]
    b, s, d = x.shape
    assert d == D_MODEL and s % ROW_TILE == 0 and s % ATTN_TILE == 0
    assert ROW_TILE % CUM_CHUNK == 0 and ROW_TILE % CHUNK == 0

    mod = _ada_call(c, ada_w, ada_b).reshape(n_layers, b, N_MOD, d)

    o_gate = 7 * D_MIX + N_HEADS
    w_in_b = w_in.astype(BF16)
    w_gate_b = w_in_b[:, :, o_gate:]
    w_a_b, w_b_b, w_c_b = w_a_out.astype(BF16), w_b_out.astype(BF16), w_c_out.astype(BF16)
    w_out_b, w1_b, w2_b = w_out.astype(BF16), mlp_w1.astype(BF16), mlp_w2.astype(BF16)

    row = lambda a: a.reshape(1, -1)
    for l in range(n_layers):
        bf = jnp.pad(fox_bf[l], (0, LANES - N_HEADS)).reshape(1, LANES)
        wsp = gmlp_ws[l].reshape(N_PAIRS, 2, CHUNK, CHUNK).transpose(0, 2, 1, 3)
        wsp = wsp.reshape(N_PAIRS, CHUNK, 2 * CHUNK).astype(BF16)
        bsp = jnp.repeat(gmlp_bs[l].T, GROUP_DIM, axis=1)
        cw = jnp.pad(conv_w[l], ((0, CONV_HALO - CONV_WIDTH), (0, 0)))

        ya, yb, qkv, frow = _mixer_in_call(
            x, mod, row(mix_pre_g[l]), w_in_b, w_b_b,
            row(gmlp_ln_g[l]), row(gmlp_ln_b[l]), wsp, bsp, cw, row(conv_b[l]),
            row(conv_ln_g[l]), row(conv_ln_b[l]), bf, l)
        yc = _attn_call(qkv, frow)
        x = _merge_call(
            x, mod, row(mix_pre_g[l]), row(mix_post_g[l]), ya, yb, yc, w_gate_b,
            w_a_b, w_c_b, w_out_b, l)
        x = _mlp_call(x, mod, row(mlp_pre_g[l]), row(mlp_post_g[l]), w1_b, w2_b, l)
    return x
```

```python
import math

import jax
import jax.numpy as jnp
from jax import lax
from jax.experimental import pallas as pl
from jax.experimental.pallas import tpu as pltpu

D_MODEL = 1024
D_MIX = 512
N_GROUPS = 8
CHUNK = 128
GROUP_DIM = D_MIX // N_GROUPS
CONV_WIDTH = 31
N_HEADS = 8
HEAD_DIM = D_MIX // N_HEADS
N_BRANCH = 3
D_FF = 4 * D_MODEL
N_MOD = 6
NORM_EPS = 1e-6

LANES = 128
SUBLANES = 8
N_PAIRS = D_MIX // LANES
CONV_HALO = 32
CUM_CHUNK = 256
VMEM_LIMIT = 56 * 1024 * 1024

ROW_TILE = 512
ATTN_TILE = 1024
STAGE_CHUNK = 512
V_ROWS = HEAD_DIM + 16
LOG2E = math.log2(math.e)

F32 = jnp.float32
BF16 = jnp.bfloat16


def _dot(a, b):
    return jnp.dot(a, b, preferred_element_type=F32)


def _sigmoid(x):
    return 0.5 * jnp.tanh(0.5 * x) + 0.5


def _gelu_tanh(x):
    c = math.sqrt(2.0 / math.pi)
    return 0.5 * x * (1.0 + jnp.tanh(c * (x + 0.044715 * (x * x * x))))


def _rms(x):
    return x * lax.rsqrt(jnp.mean(x * x, axis=-1, keepdims=True) + NORM_EPS)


def _layer_norm(x, g, b):
    mu = jnp.mean(x, axis=-1, keepdims=True)
    xc = x - mu
    var = jnp.mean(xc * xc, axis=-1, keepdims=True)
    return xc * lax.rsqrt(var + NORM_EPS) * g + b


def _split3(x):
    hi = x.astype(BF16)
    r = x - hi.astype(F32)
    mid = r.astype(BF16)
    lo = (r - mid.astype(F32)).astype(BF16)
    return hi, mid, lo


def _ada_kernel(c_ref, w_ref, b_ref, o_ref):
    c = c_ref[...]
    c_act = (c * _sigmoid(c)).astype(BF16)
    o_ref[0] = _dot(c_act, w_ref[0].astype(BF16)) + b_ref[0]


def _ada_call(c, ada_w, ada_b):
    n_layers, d, n_out = ada_w.shape
    b = c.shape[0]
    tn = 2 * D_MODEL
    return pl.pallas_call(
        _ada_kernel,
        grid=(n_layers, n_out // tn),
        in_specs=[
            pl.BlockSpec((b, d), lambda l, j: (0, 0)),
            pl.BlockSpec((1, d, tn), lambda l, j: (l, 0, j)),
            pl.BlockSpec((1, 1, tn), lambda l, j: (l, 0, j)),
        ],
        out_specs=pl.BlockSpec((1, b, tn), lambda l, j: (l, 0, j)),
        out_shape=jax.ShapeDtypeStruct((n_layers, b, n_out), F32),
        compiler_params=pltpu.CompilerParams(
            dimension_semantics=("arbitrary", "arbitrary"), vmem_limit_bytes=VMEM_LIMIT),
        name="ada_mod",
    )(c, ada_w, ada_b.reshape(n_layers, 1, n_out))


def _mixer_in_kernel(x_ref, mod_ref, g_ref, w_ref, wb_ref,
                     lng_ref, lnb_ref, wsp_ref, bsp_ref, cw_ref, cb_ref, clg_ref, clb_ref,
                     bf_ref,
                     ya_ref, yb_ref, qkv_ref, frow_ref,
                     zs_ref, sh_ref, carry_ref):
    tm = x_ref.shape[1]
    s_idx = pl.program_id(1)

    @pl.when(s_idx == 0)
    def _():
        zs_ref[0:CONV_HALO, :] = jnp.zeros((CONV_HALO, D_MIX), F32)
        carry_ref[...] = jnp.zeros_like(carry_ref)

    x = x_ref[0]
    mod = mod_ref[0, 0]
    sh1 = mod[0:1]
    sc1 = mod[1:2]
    h = (_rms(x) * (g_ref[...] * (1.0 + sc1)) + sh1).astype(BF16)

    uv_raw = _dot(h, w_ref[0, :, 0:2 * D_MIX])
    glu = _dot(h, w_ref[0, :, 2 * D_MIX:4 * D_MIX])

    uv = _gelu_tanh(uv_raw)
    u_a = uv[:, :D_MIX]
    v_a = _layer_norm(uv[:, D_MIX:], lng_ref[...], lnb_ref[...]).astype(BF16)

    zs_ref[CONV_HALO:CONV_HALO + tm, :] = glu[:, :D_MIX] * _sigmoid(glu[:, D_MIX:])

    qkv = _dot(h, w_ref[0, :, 4 * D_MIX:7 * D_MIX])
    scale = LOG2E / math.sqrt(HEAD_DIM)
    for p in range(3 * N_PAIRS):
        blk = qkv[:, p * LANES:(p + 1) * LANES]
        qkv_ref[0, p] = (blk * scale if p < N_PAIRS else blk).astype(BF16)
    f_raw = _dot(h, w_ref[0, :, 7 * D_MIX:7 * D_MIX + LANES]) + bf_ref[...]

    n_chunks = tm // CHUNK
    lane = lax.broadcasted_iota(jnp.int32, (CHUNK, LANES), 1)
    w_row = lax.broadcasted_iota(jnp.int32, (CHUNK, 2 * CHUNK), 0)
    w_col = lax.broadcasted_iota(jnp.int32, (CHUNK, 2 * CHUNK), 1)
    causal_w = (w_col % CHUNK) <= w_row
    zero_bf = jnp.zeros((CHUNK, LANES), BF16)
    for p in range(N_PAIRS):
        cols = slice(p * LANES, (p + 1) * LANES)
        w_pair = jnp.where(causal_w, wsp_ref[p], jnp.zeros_like(wsp_ref[p]))
        rhs = []
        for c in range(n_chunks):
            vc = v_a[c * CHUNK:(c + 1) * CHUNK, cols]
            rhs.append(jnp.concatenate(
                [jnp.where(lane < GROUP_DIM, vc, zero_bf),
                 jnp.where(lane >= GROUP_DIM, vc, zero_bf)], axis=0))
        sv = _dot(w_pair, jnp.concatenate(rhs, axis=1))
        bias = bsp_ref[:, cols]
        for c in range(n_chunks):
            rows = slice(c * CHUNK, (c + 1) * CHUNK)
            ya_ref[0, rows, cols] = (
                u_a[rows, cols] * (sv[:, c * LANES:(c + 1) * LANES] + bias)).astype(BF16)

    first = CONV_HALO - (CONV_WIDTH - 1)
    n_rows = tm + CONV_HALO - SUBLANES
    conv = []
    for c in range(N_PAIRS):
        cols = slice(c * LANES, (c + 1) * LANES)
        for r in range(1, SUBLANES):
            sh_ref[c, r - 1] = zs_ref[r:r + n_rows, cols]
        acc = jnp.broadcast_to(cb_ref[:, cols], (tm, LANES))
        for j in range(CONV_WIDTH):
            r = (first + j) % SUBLANES
            base = first + j - r
            src = (zs_ref[base:base + tm, cols] if r == 0
                   else sh_ref[c, r - 1, base:base + tm, :])
            acc = acc + cw_ref[j:j + 1, cols] * src
        conv.append(acc)
    zs_ref[0:CONV_HALO, :] = zs_ref[tm:tm + CONV_HALO, :]

    log_f = jnp.minimum(f_raw, 0.0) - jnp.log1p(jnp.exp(-jnp.abs(f_raw)))
    t_row = lax.broadcasted_iota(jnp.int32, (CUM_CHUNK, CUM_CHUNK), 0)
    t_col = lax.broadcasted_iota(jnp.int32, (CUM_CHUNK, CUM_CHUNK), 1)
    tri = jnp.where(t_col <= t_row, 1.0, 0.0).astype(BF16)
    carry = carry_ref[...]
    for c in range(tm // CUM_CHUNK):
        rows = slice(c * CUM_CHUNK, (c + 1) * CUM_CHUNK)
        hi, mid, lo = _split3(log_f[rows])
        cs = _dot(tri, hi) + _dot(tri, mid) + _dot(tri, lo) + carry
        carry = cs[CUM_CHUNK - 1:CUM_CHUNK, :]
        frow_ref[0, :, rows] = cs.T[0:N_HEADS, :]
    carry_ref[...] = carry

    zn = _layer_norm(jnp.concatenate(conv, axis=1), clg_ref[...], clb_ref[...])
    yb_ref[0] = _dot((zn * _sigmoid(zn)).astype(BF16), wb_ref[0])


def _mixer_in_call(x, mod, g, w_in, wb, lng, lnb, wsp, bsp, cw, cb, clg, clb, bf, layer):
    b, s, d = x.shape
    tm = ROW_TILE
    n_cols = 7 * D_MIX + LANES
    const2 = lambda bi, si: (0, 0)
    const3 = lambda bi, si: (0, 0, 0)
    row3 = lambda bi, si: (bi, si, 0)
    out_shapes = (
        jax.ShapeDtypeStruct((b, s, D_MIX), BF16),
        jax.ShapeDtypeStruct((b, s, d), F32),
        jax.ShapeDtypeStruct((b, 3 * N_PAIRS, s, LANES), BF16),
        jax.ShapeDtypeStruct((b, N_HEADS, s), F32),
    )
    return pl.pallas_call(
        _mixer_in_kernel,
        grid=(b, s // tm),
        in_specs=[
            pl.BlockSpec((1, tm, d), row3),
            pl.BlockSpec((1, 1, N_MOD, d), lambda bi, si: (layer, bi, 0, 0)),
            pl.BlockSpec((1, d), const2),
            pl.BlockSpec((1, d, n_cols), lambda bi, si: (layer, 0, 0)),
            pl.BlockSpec((1, D_MIX, d), lambda bi, si: (layer, 0, 0)),
            pl.BlockSpec((1, D_MIX), const2),
            pl.BlockSpec((1, D_MIX), const2),
            pl.BlockSpec((N_PAIRS, CHUNK, 2 * CHUNK), const3),
            pl.BlockSpec((CHUNK, D_MIX), const2),
            pl.BlockSpec((CONV_HALO, D_MIX), const2),
            pl.BlockSpec((1, D_MIX), const2),
            pl.BlockSpec((1, D_MIX), const2),
            pl.BlockSpec((1, D_MIX), const2),
            pl.BlockSpec((1, LANES), const2),
        ],
        out_specs=(
            pl.BlockSpec((1, tm, D_MIX), row3),
            pl.BlockSpec((1, tm, d), row3),
            pl.BlockSpec((1, 3 * N_PAIRS, tm, LANES), lambda bi, si: (bi, 0, si, 0)),
            pl.BlockSpec((1, N_HEADS, tm), lambda bi, si: (bi, 0, si)),
        ),
        out_shape=out_shapes,
        scratch_shapes=[
            pltpu.VMEM((CONV_HALO + tm, D_MIX), F32),
            pltpu.VMEM((N_PAIRS, SUBLANES - 1, tm + CONV_HALO - SUBLANES, LANES), F32),
            pltpu.VMEM((1, LANES), F32),
        ],
        compiler_params=pltpu.CompilerParams(
            dimension_semantics=("arbitrary", "arbitrary"), vmem_limit_bytes=VMEM_LIMIT),
        name="mixer_in",
    )(x, mod, g, w_in, wb, lng, lnb, wsp, bsp, cw, cb, clg, clb, bf)


def _aug_key_rows(f):
    n = f.shape[1]
    hi, mid, lo = _split3(f)
    sub = lax.broadcasted_iota(jnp.int32, (N_HEADS, n), 0)
    ones3 = jnp.where(sub < 3, 1.0, 0.0)
    return jnp.concatenate(
        [-hi.astype(F32), -mid.astype(F32), -lo.astype(F32), jnp.zeros((N_HEADS, n), F32),
         ones3, jnp.zeros((LANES - 5 * N_HEADS, n), F32)], axis=0)


def _aug_query_rows(f, head, row):
    hi, mid, lo = _split3(f)
    ones = (row == head) | (row == head + N_HEADS) | (row == head + 2 * N_HEADS)
    base = jnp.where(ones, 1.0, 0.0)
    return jnp.where(row == 4 * N_HEADS, hi.astype(F32),
                     jnp.where(row == 4 * N_HEADS + 1, mid.astype(F32),
                               jnp.where(row == 4 * N_HEADS + 2, lo.astype(F32), base)))


def _attn_kernel(q_ref, k_ref, v_ref, frow_ref, o_ref,
                 kk_ref, vta_ref, vtb_ref, qq_ref, sta_ref, stb_ref, mba_ref, mbb_ref,
                 m_ref, acca_ref, accb_ref):
    s_len = k_ref.shape[2]
    tq = tk = ATTN_TILE
    pair = pl.program_id(1)

    ones_rows = jnp.ones((V_ROWS - HEAD_DIM, STAGE_CHUNK), BF16)
    for c in range(s_len // STAGE_CHUNK):
        cols = slice(c * STAGE_CHUNK, (c + 1) * STAGE_CHUNK)
        kk_ref[cols, 0:LANES] = k_ref[0, 0, cols, :]
        aug_t = _aug_key_rows(frow_ref[0, :, cols] * LOG2E)
        kk_ref[cols, LANES:2 * LANES] = aug_t.T.astype(BF16)
        vt = v_ref[0, 0, cols, :].astype(F32).T
        for vt_ref, head_rows in ((vta_ref, vt[:HEAD_DIM]), (vtb_ref, vt[HEAD_DIM:])):
            vt_ref[0:HEAD_DIM, cols] = head_rows.astype(BF16)
            vt_ref[HEAD_DIM:V_ROWS, cols] = ones_rows

    n_q = s_len // tq
    row = lax.broadcasted_iota(jnp.int32, (LANES, tq), 0)
    lo_rows = row < HEAD_DIM

    for qi in range(n_q):
        q_cols = slice(qi * tq, (qi + 1) * tq)
        qt = q_ref[0, 0, q_cols, :].astype(F32).T
        top = jnp.concatenate([jnp.where(lo_rows, qt, 0.0), jnp.where(lo_rows, 0.0, qt)], axis=1)
        aug = []
        for hh in range(2):
            head = 2 * pair + hh
            f_q = frow_ref[0, pl.ds(head, 1), q_cols] * LOG2E
            aug.append(_aug_query_rows(f_q, head, row))
        qq_ref[qi] = jnp.concatenate(
            [top, jnp.concatenate(aug, axis=1)], axis=0).astype(BF16)

    half = tk // 2
    acc_refs = (acca_ref, accb_ref)
    vt_refs = (vta_ref, vtb_ref)
    bufs = ((sta_ref, mba_ref), (stb_ref, mbb_ref))

    def produce(qi, j, buf):
        st_ref, mb_ref = buf
        st = _dot(kk_ref[j * tk:(j + 1) * tk, :], qq_ref[qi])
        st_ref[...] = st
        mb_ref[...] = jnp.max(st, axis=0, keepdims=True)

    def produce_diag(qi, j, buf):
        st_ref, mb_ref = buf
        start = j * tk
        r = lax.broadcasted_iota(jnp.int32, (half, 2 * tq), 0)
        c = lax.broadcasted_iota(jnp.int32, (half, 2 * tq), 1)
        st = _dot(kk_ref[start:start + half, :], qq_ref[qi])
        st = jnp.where(r <= jnp.where(c >= tq, c - tq, c), st, -jnp.inf)
        st_ref[0:half, :] = st
        mb_ref[...] = jnp.max(st, axis=0, keepdims=True)
        qq_late = jnp.concatenate([qq_ref[qi, :, half:tq], qq_ref[qi, :, tq + half:2 * tq]], axis=1)
        r = lax.broadcasted_iota(jnp.int32, (half, tq), 0)
        c = lax.broadcasted_iota(jnp.int32, (half, tq), 1)
        st = _dot(kk_ref[start + half:start + tk, :], qq_late)
        st_ref[half:tk, 0:tq] = jnp.where(r <= jnp.where(c >= half, c - half, c), st, -jnp.inf)

    def consume(j, buf):
        st_ref, mb_ref = buf
        m_old = m_ref[...]
        m_new = jnp.maximum(m_old, mb_ref[...])
        alpha = jnp.exp2(m_old - m_new)
        m_ref[...] = m_new
        pt = jnp.exp2(st_ref[...] - m_new).astype(BF16)
        for hh in range(2):
            cols = slice(hh * tq, (hh + 1) * tq)
            acc_refs[hh][...] = (alpha[:, cols] * acc_refs[hh][...]
                                 + _dot(vt_refs[hh][:, j * tk:(j + 1) * tk], pt[:, cols]))

    def consume_diag(j, buf):
        st_ref, mb_ref = buf
        start = j * tk
        m_old = m_ref[...]
        m_new = jnp.maximum(m_old, mb_ref[...])
        alpha = jnp.exp2(m_old - m_new)
        m_ref[...] = m_new
        pt = jnp.exp2(st_ref[0:half, :] - m_new).astype(BF16)
        for hh in range(2):
            cols = slice(hh * tq, (hh + 1) * tq)
            acc_refs[hh][...] = (alpha[:, cols] * acc_refs[hh][...]
                                 + _dot(vt_refs[hh][:, start:start + half], pt[:, cols]))
        for hh in range(2):
            cols = slice(hh * tq + half, (hh + 1) * tq)
            st = st_ref[half:tk, hh * half:(hh + 1) * half]
            m_old = m_ref[:, cols]
            m_new = jnp.maximum(m_old, jnp.max(st, axis=0, keepdims=True))
            alpha = jnp.exp2(m_old - m_new)
            m_ref[:, cols] = m_new
            pt = jnp.exp2(st - m_new).astype(BF16)
            acc_refs[hh][:, half:tq] = (alpha * acc_refs[hh][:, half:tq]
                                        + _dot(vt_refs[hh][:, start + half:start + tk], pt))

    blocks = [(qi, j) for qi in range(n_q) for j in range(qi + 1)]

    def issue(t):
        qi, j = blocks[t]
        (produce_diag if j == qi else produce)(qi, j, bufs[t % 2])

    issue(0)
    for t, (qi, j) in enumerate(blocks):
        if t + 1 < len(blocks):
            issue(t + 1)
        if j == 0:
            m_ref[...] = jnp.full(m_ref.shape, -1e30, F32)
            acca_ref[...] = jnp.zeros_like(acca_ref)
            accb_ref[...] = jnp.zeros_like(accb_ref)
        (consume_diag if j == qi else consume)(j, bufs[t % 2])
        if j == qi:
            ot = jnp.concatenate(
                [acc_ref[0:HEAD_DIM, :] * (1.0 / acc_ref[HEAD_DIM:HEAD_DIM + 1, :])
                 for acc_ref in acc_refs], axis=0)
            o_ref[0, 0, qi * tq:(qi + 1) * tq, :] = ot.T.astype(BF16)


def _attn_call(qkv, frow):
    b, _, s, _ = qkv.shape
    tq = ATTN_TILE
    return pl.pallas_call(
        _attn_kernel,
        grid=(b, N_PAIRS),
        in_specs=[
            pl.BlockSpec((1, 1, s, LANES), lambda bi, p: (bi, p, 0, 0)),
            pl.BlockSpec((1, 1, s, LANES), lambda bi, p: (bi, N_PAIRS + p, 0, 0)),
            pl.BlockSpec((1, 1, s, LANES), lambda bi, p: (bi, 2 * N_PAIRS + p, 0, 0)),
            pl.BlockSpec((1, N_HEADS, s), lambda bi, p: (bi, 0, 0)),
        ],
        out_specs=pl.BlockSpec((1, 1, s, LANES), lambda bi, p: (bi, p, 0, 0)),
        out_shape=jax.ShapeDtypeStruct((b, N_PAIRS, s, LANES), BF16),
        scratch_shapes=[
            pltpu.VMEM((s, 2 * LANES), BF16),
            pltpu.VMEM((V_ROWS, s), BF16),
            pltpu.VMEM((V_ROWS, s), BF16),
            pltpu.VMEM((s // tq, 2 * LANES, 2 * tq), BF16),
            pltpu.VMEM((tq, 2 * tq), F32),
            pltpu.VMEM((tq, 2 * tq), F32),
            pltpu.VMEM((1, 2 * tq), F32),
            pltpu.VMEM((1, 2 * tq), F32),
            pltpu.VMEM((1, 2 * tq), F32),
            pltpu.VMEM((V_ROWS, tq), F32),
            pltpu.VMEM((V_ROWS, tq), F32),
        ],
        compiler_params=pltpu.CompilerParams(
            dimension_semantics=("arbitrary", "arbitrary"), vmem_limit_bytes=VMEM_LIMIT),
        name="fox_attn",
    )(qkv, qkv, qkv, frow)


def _merge_kernel(x_ref, mod_ref, gpre_ref, gpost_ref, ya_ref, yb_ref, yc_ref,
                  wgate_ref, wa_ref, wc_ref, wout_ref, o_ref):
    x = x_ref[0]
    mod = mod_ref[0, 0]
    sh1 = mod[0:1]
    sc1 = mod[1:2]
    gt1 = mod[2:3]
    h = (_rms(x) * (gpre_ref[...] * (1.0 + sc1)) + sh1).astype(BF16)

    def gate(br):
        return _sigmoid(_dot(h, wgate_ref[0, :, br * D_MODEL:(br + 1) * D_MODEL]))

    merged = (gate(0) * _dot(ya_ref[0], wa_ref[0]) + gate(1) * yb_ref[0]
              + gate(2) * _dot(jnp.concatenate([yc_ref[0, p] for p in range(N_PAIRS)], axis=1),
                               wc_ref[0]))
    y = _dot(merged.astype(BF16), wout_ref[0])
    o_ref[0] = x + gt1 * (_rms(y) * gpost_ref[...])


def _merge_call(x, mod, gpre, gpost, ya, yb, yc, wgate, wa, wc, wout, layer):
    b, s, d = x.shape
    tm = ROW_TILE
    const2 = lambda bi, si: (0, 0)
    lay3 = lambda bi, si: (layer, 0, 0)
    row3 = lambda bi, si: (bi, si, 0)
    return pl.pallas_call(
        _merge_kernel,
        grid=(b, s // tm),
        in_specs=[
            pl.BlockSpec((1, tm, d), row3),
            pl.BlockSpec((1, 1, N_MOD, d), lambda bi, si: (layer, bi, 0, 0)),
            pl.BlockSpec((1, d), const2),
            pl.BlockSpec((1, d), const2),
            pl.BlockSpec((1, tm, D_MIX), row3),
            pl.BlockSpec((1, tm, d), row3),
            pl.BlockSpec((1, N_PAIRS, tm, LANES), lambda bi, si: (bi, 0, si, 0)),
            pl.BlockSpec((1, d, N_BRANCH * d), lay3),
            pl.BlockSpec((1, D_MIX, d), lay3),
            pl.BlockSpec((1, D_MIX, d), lay3),
            pl.BlockSpec((1, d, d), lay3),
        ],
        out_specs=pl.BlockSpec((1, tm, d), row3),
        out_shape=jax.ShapeDtypeStruct((b, s, d), F32),
        compiler_params=pltpu.CompilerParams(
            dimension_semantics=("arbitrary", "arbitrary"), vmem_limit_bytes=VMEM_LIMIT),
        name="mixer_merge",
    )(x, mod, gpre, gpost, ya, yb, yc, wgate, wa, wc, wout)


def _mlp_kernel(x_ref, mod_ref, gpre_ref, gpost_ref, w1_ref, w2_ref, o_ref):
    x = x_ref[0]
    mod = mod_ref[0, 0]
    sh2 = mod[3:4]
    sc2 = mod[4:5]
    gt2 = mod[5:6]
    h = (_rms(x) * (gpre_ref[...] * (1.0 + sc2)) + sh2).astype(BF16)
    y = None
    for c in range(D_FF // D_MODEL):
        cols = slice(c * D_MODEL, (c + 1) * D_MODEL)
        a = jnp.maximum(_dot(h, w1_ref[0, :, cols]), 0.0)
        t = _dot((a * a).astype(BF16), w2_ref[0, cols, :])
        y = t if y is None else y + t
    o_ref[0] = x + gt2 * (_rms(y) * gpost_ref[...])


def _mlp_call(x, mod, gpre, gpost, w1, w2, layer):
    b, s, d = x.shape
    tm = ROW_TILE
    const2 = lambda bi, si: (0, 0)
    lay3 = lambda bi, si: (layer, 0, 0)
    row3 = lambda bi, si: (bi, si, 0)
    return pl.pallas_call(
        _mlp_kernel,
        grid=(b, s // tm),
        in_specs=[
            pl.BlockSpec((1, tm, d), row3),
            pl.BlockSpec((1, 1, N_MOD, d), lambda bi, si: (layer, bi, 0, 0)),
            pl.BlockSpec((1, d), const2),
            pl.BlockSpec((1, d), const2),
            pl.BlockSpec((1, d, D_FF), lay3),
            pl.BlockSpec((1, D_FF, d), lay3),
        ],
        out_specs=pl.BlockSpec((1, tm, d), row3),
        out_shape=jax.ShapeDtypeStruct((b, s, d), F32),
        compiler_params=pltpu.CompilerParams(
            dimension_semantics=("arbitrary", "arbitrary"), vmem_limit_bytes=VMEM_LIMIT),
        name="relu2_mlp",
    )(x, mod, gpre, gpost, w1, w2)


def kernel(x, c, ada_w, ada_b, mix_pre_g, mix_post_g, mlp_pre_g, mlp_post_g, w_in, gmlp_ln_g, gmlp_ln_b, gmlp_ws, gmlp_bs, w_a_out, conv_w, conv_b, conv_ln_g, conv_ln_b, w_b_out, fox_bf, w_c_out, w_out, mlp_w1, mlp_w2):
    n_layers = ada_w.shape[0]
    b, s, d = x.shape
    assert d == D_MODEL and s % ROW_TILE == 0 and s % ATTN_TILE == 0
    assert ROW_TILE % CUM_CHUNK == 0 and ROW_TILE % CHUNK == 0

    mod = _ada_call(c, ada_w, ada_b).reshape(n_layers, b, N_MOD, d)

    o_gate = 7 * D_MIX + N_HEADS
    w_in_b = w_in.astype(BF16)
    w_gate_b = w_in_b[:, :, o_gate:]
    w_a_b, w_b_b, w_c_b = w_a_out.astype(BF16), w_b_out.astype(BF16), w_c_out.astype(BF16)
    w_out_b, w1_b, w2_b = w_out.astype(BF16), mlp_w1.astype(BF16), mlp_w2.astype(BF16)

    row = lambda a: a.reshape(1, -1)
    for l in range(n_layers):
        bf = jnp.pad(fox_bf[l], (0, LANES - N_HEADS)).reshape(1, LANES)
        wsp = gmlp_ws[l].reshape(N_PAIRS, 2, CHUNK, CHUNK).transpose(0, 2, 1, 3)
        wsp = wsp.reshape(N_PAIRS, CHUNK, 2 * CHUNK).astype(BF16)
        bsp = jnp.repeat(gmlp_bs[l].T, GROUP_DIM, axis=1)
        cw = jnp.pad(conv_w[l], ((0, CONV_HALO - CONV_WIDTH), (0, 0)))

        ya, yb, qkv, frow = _mixer_in_call(
            x, mod, row(mix_pre_g[l]), w_in_b, w_b_b,
            row(gmlp_ln_g[l]), row(gmlp_ln_b[l]), wsp, bsp, cw, row(conv_b[l]),
            row(conv_ln_g[l]), row(conv_ln_b[l]), bf, l)
        yc = _attn_call(qkv, frow)
        x = _merge_call(
            x, mod, row(mix_pre_g[l]), row(mix_post_g[l]), ya, yb, yc, w_gate_b,
            w_a_b, w_c_b, w_out_b, l)
        x = _mlp_call(x, mod, row(mlp_pre_g[l]), row(mlp_post_g[l]), w1_b, w2_b, l)
    return x
```

```python
import math

import jax
import jax.numpy as jnp
from jax import lax
from jax.experimental import pallas as pl
from jax.experimental.pallas import tpu as pltpu

D_MODEL = 1024
D_MIX = 512
N_GROUPS = 8
CHUNK = 128
GROUP_DIM = D_MIX // N_GROUPS
CONV_WIDTH = 31
N_HEADS = 8
HEAD_DIM = D_MIX // N_HEADS
N_BRANCH = 3
D_FF = 4 * D_MODEL
N_MOD = 6
NORM_EPS = 1e-6

LANES = 128
SUBLANES = 8
N_PAIRS = D_MIX // LANES
CONV_HALO = 32
CUM_CHUNK = 256
VMEM_LIMIT = 56 * 1024 * 1024

ROW_TILE = 512
ATTN_TILE = 1024
STAGE_CHUNK = 512
V_ROWS = HEAD_DIM + 16
LOG2E = math.log2(math.e)

F32 = jnp.float32
BF16 = jnp.bfloat16


def _dot(a, b):
    return jnp.dot(a, b, preferred_element_type=F32)


def _sigmoid(x):
    return 0.5 * jnp.tanh(0.5 * x) + 0.5


def _gelu_tanh(x):
    c = math.sqrt(2.0 / math.pi)
    return 0.5 * x * (1.0 + jnp.tanh(c * (x + 0.044715 * (x * x * x))))


def _rms(x):
    return x * lax.rsqrt(jnp.mean(x * x, axis=-1, keepdims=True) + NORM_EPS)


def _layer_norm(x, g, b):
    mu = jnp.mean(x, axis=-1, keepdims=True)
    xc = x - mu
    var = jnp.mean(xc * xc, axis=-1, keepdims=True)
    return xc * lax.rsqrt(var + NORM_EPS) * g + b


def _split3(x):
    hi = x.astype(BF16)
    r = x - hi.astype(F32)
    mid = r.astype(BF16)
    lo = (r - mid.astype(F32)).astype(BF16)
    return hi, mid, lo


def _ada_kernel(c_ref, w_ref, b_ref, o_ref):
    c = c_ref[...]
    c_act = (c * _sigmoid(c)).astype(BF16)
    o_ref[0] = _dot(c_act, w_ref[0].astype(BF16)) + b_ref[0]


def _ada_call(c, ada_w, ada_b):
    n_layers, d, n_out = ada_w.shape
    b = c.shape[0]
    tn = 2 * D_MODEL
    return pl.pallas_call(
        _ada_kernel,
        grid=(n_layers, n_out // tn),
        in_specs=[
            pl.BlockSpec((b, d), lambda l, j: (0, 0)),
            pl.BlockSpec((1, d, tn), lambda l, j: (l, 0, j)),
            pl.BlockSpec((1, 1, tn), lambda l, j: (l, 0, j)),
        ],
        out_specs=pl.BlockSpec((1, b, tn), lambda l, j: (l, 0, j)),
        out_shape=jax.ShapeDtypeStruct((n_layers, b, n_out), F32),
        compiler_params=pltpu.CompilerParams(
            dimension_semantics=("arbitrary", "arbitrary"), vmem_limit_bytes=VMEM_LIMIT),
        name="ada_mod",
    )(c, ada_w, ada_b.reshape(n_layers, 1, n_out))


def _mixer_in_kernel(x_ref, mod_ref, g_ref, w_ref, wb_ref,
                     lng_ref, lnb_ref, wsp_ref, bsp_ref, cw_ref, cb_ref, clg_ref, clb_ref,
                     bf_ref,
                     ya_ref, yb_ref, qkv_ref, frow_ref,
                     uv_ref, f_ref, zs_ref, sh_ref, carry_ref):
    tm = x_ref.shape[1]
    hr = tm // 2
    s_idx = pl.program_id(1)

    @pl.when(s_idx == 0)
    def _():
        zs_ref[0:CONV_HALO, :] = jnp.zeros((CONV_HALO, D_MIX), F32)
        carry_ref[...] = jnp.zeros_like(carry_ref)

    mod = mod_ref[0, 0]
    sh1 = mod[0:1]
    sc1 = mod[1:2]
    scale = LOG2E / math.sqrt(HEAD_DIM)
    first = CONV_HALO - (CONV_WIDTH - 1)

    def project(hf):
        rows = slice(hf * hr, (hf + 1) * hr)
        h = (_rms(x_ref[0, rows, :]) * (g_ref[...] * (1.0 + sc1)) + sh1).astype(BF16)
        uv_ref[hf] = _dot(h, w_ref[0, :, 0:2 * D_MIX])
        glu = _dot(h, w_ref[0, :, 2 * D_MIX:4 * D_MIX])
        zs_ref[CONV_HALO + hf * hr:CONV_HALO + (hf + 1) * hr, :] = (
            glu[:, :D_MIX] * _sigmoid(glu[:, D_MIX:]))
        qkv = _dot(h, w_ref[0, :, 4 * D_MIX:7 * D_MIX])
        for p in range(3 * N_PAIRS):
            blk = qkv[:, p * LANES:(p + 1) * LANES]
            qkv_ref[0, p, rows, :] = (blk * scale if p < N_PAIRS else blk).astype(BF16)
        f_ref[hf] = _dot(h, w_ref[0, :, 7 * D_MIX:7 * D_MIX + LANES]) + bf_ref[...]

    def mix(hf):
        row0 = hf * hr
        uv = _gelu_tanh(uv_ref[hf])
        u_a = uv[:, :D_MIX]
        v_a = _layer_norm(uv[:, D_MIX:], lng_ref[...], lnb_ref[...]).astype(BF16)
        n_chunks = hr // CHUNK
        lane = lax.broadcasted_iota(jnp.int32, (CHUNK, LANES), 1)
        w_row = lax.broadcasted_iota(jnp.int32, (CHUNK, 2 * CHUNK), 0)
        w_col = lax.broadcasted_iota(jnp.int32, (CHUNK, 2 * CHUNK), 1)
        causal_w = (w_col % CHUNK) <= w_row
        zero_bf = jnp.zeros((CHUNK, LANES), BF16)
        for p in range(N_PAIRS):
            cols = slice(p * LANES, (p + 1) * LANES)
            w_pair = jnp.where(causal_w, wsp_ref[p], jnp.zeros_like(wsp_ref[p]))
            rhs = []
            for c in range(n_chunks):
                vc = v_a[c * CHUNK:(c + 1) * CHUNK, cols]
                rhs.append(jnp.concatenate(
                    [jnp.where(lane < GROUP_DIM, vc, zero_bf),
                     jnp.where(lane >= GROUP_DIM, vc, zero_bf)], axis=0))
            sv = _dot(w_pair, jnp.concatenate(rhs, axis=1))
            bias = bsp_ref[:, cols]
            for c in range(n_chunks):
                rows = slice(c * CHUNK, (c + 1) * CHUNK)
                ya_ref[0, row0 + c * CHUNK:row0 + (c + 1) * CHUNK, cols] = (
                    u_a[rows, cols] * (sv[:, c * LANES:(c + 1) * LANES] + bias)).astype(BF16)

        n_rows = hr + CONV_HALO - SUBLANES
        conv = []
        for c in range(N_PAIRS):
            cols = slice(c * LANES, (c + 1) * LANES)
            for r in range(1, SUBLANES):
                sh_ref[c, r - 1, row0:row0 + n_rows, :] = zs_ref[row0 + r:row0 + r + n_rows, cols]
            acc = jnp.broadcast_to(cb_ref[:, cols], (hr, LANES))
            for j in range(CONV_WIDTH):
                r = (first + j) % SUBLANES
                base = row0 + first + j - r
                src = (zs_ref[base:base + hr, cols] if r == 0
                       else sh_ref[c, r - 1, base:base + hr, :])
                acc = acc + cw_ref[j:j + 1, cols] * src
            conv.append(acc)
        zn = _layer_norm(jnp.concatenate(conv, axis=1), clg_ref[...], clb_ref[...])
        yb_ref[0, row0:row0 + hr, :] = _dot((zn * _sigmoid(zn)).astype(BF16), wb_ref[0])

        f_raw = f_ref[hf]
        log_f = jnp.minimum(f_raw, 0.0) - jnp.log1p(jnp.exp(-jnp.abs(f_raw)))
        t_row = lax.broadcasted_iota(jnp.int32, (CUM_CHUNK, CUM_CHUNK), 0)
        t_col = lax.broadcasted_iota(jnp.int32, (CUM_CHUNK, CUM_CHUNK), 1)
        tri = jnp.where(t_col <= t_row, 1.0, 0.0).astype(BF16)
        carry = carry_ref[...]
        for c in range(hr // CUM_CHUNK):
            rows = slice(c * CUM_CHUNK, (c + 1) * CUM_CHUNK)
            hi, mid, lo = _split3(log_f[rows])
            cs = _dot(tri, hi) + _dot(tri, mid) + _dot(tri, lo) + carry
            carry = cs[CUM_CHUNK - 1:CUM_CHUNK, :]
            frow_ref[0, :, row0 + c * CUM_CHUNK:row0 + (c + 1) * CUM_CHUNK] = cs.T[0:N_HEADS, :]
        carry_ref[...] = carry

    project(0)
    project(1)
    mix(0)
    mix(1)
    zs_ref[0:CONV_HALO, :] = zs_ref[tm:tm + CONV_HALO, :]


def _mixer_in_call(x, mod, g, w_in, wb, lng, lnb, wsp, bsp, cw, cb, clg, clb, bf, layer):
    b, s, d = x.shape
    tm = ROW_TILE
    n_cols = 7 * D_MIX + LANES
    const2 = lambda bi, si: (0, 0)
    const3 = lambda bi, si: (0, 0, 0)
    row3 = lambda bi, si: (bi, si, 0)
    out_shapes = (
        jax.ShapeDtypeStruct((b, s, D_MIX), BF16),
        jax.ShapeDtypeStruct((b, s, d), F32),
        jax.ShapeDtypeStruct((b, 3 * N_PAIRS, s, LANES), BF16),
        jax.ShapeDtypeStruct((b, N_HEADS, s), F32),
    )
    return pl.pallas_call(
        _mixer_in_kernel,
        grid=(b, s // tm),
        in_specs=[
            pl.BlockSpec((1, tm, d), row3),
            pl.BlockSpec((1, 1, N_MOD, d), lambda bi, si: (layer, bi, 0, 0)),
            pl.BlockSpec((1, d), const2),
            pl.BlockSpec((1, d, n_cols), lambda bi, si: (layer, 0, 0)),
            pl.BlockSpec((1, D_MIX, d), lambda bi, si: (layer, 0, 0)),
            pl.BlockSpec((1, D_MIX), const2),
            pl.BlockSpec((1, D_MIX), const2),
            pl.BlockSpec((N_PAIRS, CHUNK, 2 * CHUNK), const3),
            pl.BlockSpec((CHUNK, D_MIX), const2),
            pl.BlockSpec((CONV_HALO, D_MIX), const2),
            pl.BlockSpec((1, D_MIX), const2),
            pl.BlockSpec((1, D_MIX), const2),
            pl.BlockSpec((1, D_MIX), const2),
            pl.BlockSpec((1, LANES), const2),
        ],
        out_specs=(
            pl.BlockSpec((1, tm, D_MIX), row3),
            pl.BlockSpec((1, tm, d), row3),
            pl.BlockSpec((1, 3 * N_PAIRS, tm, LANES), lambda bi, si: (bi, 0, si, 0)),
            pl.BlockSpec((1, N_HEADS, tm), lambda bi, si: (bi, 0, si)),
        ),
        out_shape=out_shapes,
        scratch_shapes=[
            pltpu.VMEM((2, tm // 2, 2 * D_MIX), F32),
            pltpu.VMEM((2, tm // 2, LANES), F32),
            pltpu.VMEM((CONV_HALO + tm, D_MIX), F32),
            pltpu.VMEM((N_PAIRS, SUBLANES - 1, tm + CONV_HALO - SUBLANES, LANES), F32),
            pltpu.VMEM((1, LANES), F32),
        ],
        compiler_params=pltpu.CompilerParams(
            dimension_semantics=("arbitrary", "arbitrary"), vmem_limit_bytes=VMEM_LIMIT),
        name="mixer_in",
    )(x, mod, g, w_in, wb, lng, lnb, wsp, bsp, cw, cb, clg, clb, bf)


def _aug_key_rows(f):
    n = f.shape[1]
    hi, mid, lo = _split3(f)
    sub = lax.broadcasted_iota(jnp.int32, (N_HEADS, n), 0)
    ones3 = jnp.where(sub < 3, 1.0, 0.0)
    return jnp.concatenate(
        [-hi.astype(F32), -mid.astype(F32), -lo.astype(F32), jnp.zeros((N_HEADS, n), F32),
         ones3, jnp.zeros((LANES - 5 * N_HEADS, n), F32)], axis=0)


def _aug_query_rows(f, head, row):
    hi, mid, lo = _split3(f)
    ones = (row == head) | (row == head + N_HEADS) | (row == head + 2 * N_HEADS)
    base = jnp.where(ones, 1.0, 0.0)
    return jnp.where(row == 4 * N_HEADS, hi.astype(F32),
                     jnp.where(row == 4 * N_HEADS + 1, mid.astype(F32),
                               jnp.where(row == 4 * N_HEADS + 2, lo.astype(F32), base)))


def _attn_kernel(q_ref, k_ref, v_ref, frow_ref, o_ref,
                 kk_ref, vta_ref, vtb_ref, qq_ref, sta_ref, stb_ref, mba_ref, mbb_ref,
                 m_ref, acca_ref, accb_ref):
    s_len = k_ref.shape[2]
    tq = tk = ATTN_TILE
    pair = pl.program_id(1)

    ones_rows = jnp.ones((V_ROWS - HEAD_DIM, STAGE_CHUNK), BF16)
    for c in range(s_len // STAGE_CHUNK):
        cols = slice(c * STAGE_CHUNK, (c + 1) * STAGE_CHUNK)
        kk_ref[cols, 0:LANES] = k_ref[0, 0, cols, :]
        aug_t = _aug_key_rows(frow_ref[0, :, cols] * LOG2E)
        kk_ref[cols, LANES:2 * LANES] = aug_t.T.astype(BF16)
        vt = v_ref[0, 0, cols, :].astype(F32).T
        for vt_ref, head_rows in ((vta_ref, vt[:HEAD_DIM]), (vtb_ref, vt[HEAD_DIM:])):
            vt_ref[0:HEAD_DIM, cols] = head_rows.astype(BF16)
            vt_ref[HEAD_DIM:V_ROWS, cols] = ones_rows

    n_q = s_len // tq
    row = lax.broadcasted_iota(jnp.int32, (LANES, tq), 0)
    lo_rows = row < HEAD_DIM

    for qi in range(n_q):
        q_cols = slice(qi * tq, (qi + 1) * tq)
        qt = q_ref[0, 0, q_cols, :].astype(F32).T
        top = jnp.concatenate([jnp.where(lo_rows, qt, 0.0), jnp.where(lo_rows, 0.0, qt)], axis=1)
        aug = []
        for hh in range(2):
            head = 2 * pair + hh
            f_q = frow_ref[0, pl.ds(head, 1), q_cols] * LOG2E
            aug.append(_aug_query_rows(f_q, head, row))
        qq_ref[qi] = jnp.concatenate(
            [top, jnp.concatenate(aug, axis=1)], axis=0).astype(BF16)

    half = tk // 2
    acc_refs = (acca_ref, accb_ref)
    vt_refs = (vta_ref, vtb_ref)
    bufs = ((sta_ref, mba_ref), (stb_ref, mbb_ref))

    def produce(qi, j, buf):
        st_ref, mb_ref = buf
        st = _dot(kk_ref[j * tk:(j + 1) * tk, :], qq_ref[qi])
        st_ref[...] = st
        mb_ref[...] = jnp.max(st, axis=0, keepdims=True)

    def produce_diag(qi, j, buf):
        st_ref, mb_ref = buf
        start = j * tk
        r = lax.broadcasted_iota(jnp.int32, (half, 2 * tq), 0)
        c = lax.broadcasted_iota(jnp.int32, (half, 2 * tq), 1)
        st = _dot(kk_ref[start:start + half, :], qq_ref[qi])
        st = jnp.where(r <= jnp.where(c >= tq, c - tq, c), st, -jnp.inf)
        st_ref[0:half, :] = st
        mb_ref[...] = jnp.max(st, axis=0, keepdims=True)
        qq_late = jnp.concatenate([qq_ref[qi, :, half:tq], qq_ref[qi, :, tq + half:2 * tq]], axis=1)
        r = lax.broadcasted_iota(jnp.int32, (half, tq), 0)
        c = lax.broadcasted_iota(jnp.int32, (half, tq), 1)
        st = _dot(kk_ref[start + half:start + tk, :], qq_late)
        st_ref[half:tk, 0:tq] = jnp.where(r <= jnp.where(c >= half, c - half, c), st, -jnp.inf)

    def consume(j, buf):
        st_ref, mb_ref = buf
        m_old = m_ref[...]
        m_new = jnp.maximum(m_old, mb_ref[...])
        alpha = jnp.exp2(m_old - m_new)
        m_ref[...] = m_new
        pt = jnp.exp2(st_ref[...] - m_new).astype(BF16)
        for hh in range(2):
            cols = slice(hh * tq, (hh + 1) * tq)
            acc_refs[hh][...] = (alpha[:, cols] * acc_refs[hh][...]
                                 + _dot(vt_refs[hh][:, j * tk:(j + 1) * tk], pt[:, cols]))

    def consume_diag(j, buf):
        st_ref, mb_ref = buf
        start = j * tk
        m_old = m_ref[...]
        m_new = jnp.maximum(m_old, mb_ref[...])
        alpha = jnp.exp2(m_old - m_new)
        m_ref[...] = m_new
        pt = jnp.exp2(st_ref[0:half, :] - m_new).astype(BF16)
        for hh in range(2):
            cols = slice(hh * tq, (hh + 1) * tq)
            acc_refs[hh][...] = (alpha[:, cols] * acc_refs[hh][...]
                                 + _dot(vt_refs[hh][:, start:start + half], pt[:, cols]))
        for hh in range(2):
            cols = slice(hh * tq + half, (hh + 1) * tq)
            st = st_ref[half:tk, hh * half:(hh + 1) * half]
            m_old = m_ref[:, cols]
            m_new = jnp.maximum(m_old, jnp.max(st, axis=0, keepdims=True))
            alpha = jnp.exp2(m_old - m_new)
            m_ref[:, cols] = m_new
            pt = jnp.exp2(st - m_new).astype(BF16)
            acc_refs[hh][:, half:tq] = (alpha * acc_refs[hh][:, half:tq]
                                        + _dot(vt_refs[hh][:, start + half:start + tk], pt))

    blocks = [(qi, j) for qi in range(n_q) for j in range(qi + 1)]

    def issue(t):
        qi, j = blocks[t]
        (produce_diag if j == qi else produce)(qi, j, bufs[t % 2])

    issue(0)
    for t, (qi, j) in enumerate(blocks):
        if t + 1 < len(blocks):
            issue(t + 1)
        if j == 0:
            m_ref[...] = jnp.full(m_ref.shape, -1e30, F32)
            acca_ref[...] = jnp.zeros_like(acca_ref)
            accb_ref[...] = jnp.zeros_like(accb_ref)
        (consume_diag if j == qi else consume)(j, bufs[t % 2])
        if j == qi:
            ot = jnp.concatenate(
                [acc_ref[0:HEAD_DIM, :] * (1.0 / acc_ref[HEAD_DIM:HEAD_DIM + 1, :])
                 for acc_ref in acc_refs], axis=0)
            o_ref[0, 0, qi * tq:(qi + 1) * tq, :] = ot.T.astype(BF16)


def _attn_call(qkv, frow):
    b, _, s, _ = qkv.shape
    tq = ATTN_TILE
    return pl.pallas_call(
        _attn_kernel,
        grid=(b, N_PAIRS),
        in_specs=[
            pl.BlockSpec((1, 1, s, LANES), lambda bi, p: (bi, p, 0, 0)),
            pl.BlockSpec((1, 1, s, LANES), lambda bi, p: (bi, N_PAIRS + p, 0, 0)),
            pl.BlockSpec((1, 1, s, LANES), lambda bi, p: (bi, 2 * N_PAIRS + p, 0, 0)),
            pl.BlockSpec((1, N_HEADS, s), lambda bi, p: (bi, 0, 0)),
        ],
        out_specs=pl.BlockSpec((1, 1, s, LANES), lambda bi, p: (bi, p, 0, 0)),
        out_shape=jax.ShapeDtypeStruct((b, N_PAIRS, s, LANES), BF16),
        scratch_shapes=[
            pltpu.VMEM((s, 2 * LANES), BF16),
            pltpu.VMEM((V_ROWS, s), BF16),
            pltpu.VMEM((V_ROWS, s), BF16),
            pltpu.VMEM((s // tq, 2 * LANES, 2 * tq), BF16),
            pltpu.VMEM((tq, 2 * tq), F32),
            pltpu.VMEM((tq, 2 * tq), F32),
            pltpu.VMEM((1, 2 * tq), F32),
            pltpu.VMEM((1, 2 * tq), F32),
            pltpu.VMEM((1, 2 * tq), F32),
            pltpu.VMEM((V_ROWS, tq), F32),
            pltpu.VMEM((V_ROWS, tq), F32),
        ],
        compiler_params=pltpu.CompilerParams(
            dimension_semantics=("arbitrary", "arbitrary"), vmem_limit_bytes=VMEM_LIMIT),
        name="fox_attn",
    )(qkv, qkv, qkv, frow)


def _merge_kernel(x_ref, mod_ref, gpre_ref, gpost_ref, ya_ref, yb_ref, yc_ref,
                  wgate_ref, wa_ref, wc_ref, wout_ref, o_ref):
    x = x_ref[0]
    mod = mod_ref[0, 0]
    sh1 = mod[0:1]
    sc1 = mod[1:2]
    gt1 = mod[2:3]
    h = (_rms(x) * (gpre_ref[...] * (1.0 + sc1)) + sh1).astype(BF16)

    def gate(br):
        return _sigmoid(_dot(h, wgate_ref[0, :, br * D_MODEL:(br + 1) * D_MODEL]))

    merged = (gate(0) * _dot(ya_ref[0], wa_ref[0]) + gate(1) * yb_ref[0]
              + gate(2) * _dot(jnp.concatenate([yc_ref[0, p] for p in range(N_PAIRS)], axis=1),
                               wc_ref[0]))
    y = _dot(merged.astype(BF16), wout_ref[0])
    o_ref[0] = x + gt1 * (_rms(y) * gpost_ref[...])


def _merge_call(x, mod, gpre, gpost, ya, yb, yc, wgate, wa, wc, wout, layer):
    b, s, d = x.shape
    tm = ROW_TILE
    const2 = lambda bi, si: (0, 0)
    lay3 = lambda bi, si: (layer, 0, 0)
    row3 = lambda bi, si: (bi, si, 0)
    return pl.pallas_call(
        _merge_kernel,
        grid=(b, s // tm),
        in_specs=[
            pl.BlockSpec((1, tm, d), row3),
            pl.BlockSpec((1, 1, N_MOD, d), lambda bi, si: (layer, bi, 0, 0)),
            pl.BlockSpec((1, d), const2),
            pl.BlockSpec((1, d), const2),
            pl.BlockSpec((1, tm, D_MIX), row3),
            pl.BlockSpec((1, tm, d), row3),
            pl.BlockSpec((1, N_PAIRS, tm, LANES), lambda bi, si: (bi, 0, si, 0)),
            pl.BlockSpec((1, d, N_BRANCH * d), lay3),
            pl.BlockSpec((1, D_MIX, d), lay3),
            pl.BlockSpec((1, D_MIX, d), lay3),
            pl.BlockSpec((1, d, d), lay3),
        ],
        out_specs=pl.BlockSpec((1, tm, d), row3),
        out_shape=jax.ShapeDtypeStruct((b, s, d), F32),
        compiler_params=pltpu.CompilerParams(
            dimension_semantics=("arbitrary", "arbitrary"), vmem_limit_bytes=VMEM_LIMIT),
        name="mixer_merge",
    )(x, mod, gpre, gpost, ya, yb, yc, wgate, wa, wc, wout)


def _mlp_kernel(x_ref, mod_ref, gpre_ref, gpost_ref, w1_ref, w2_ref, o_ref):
    x = x_ref[0]
    mod = mod_ref[0, 0]
    sh2 = mod[3:4]
    sc2 = mod[4:5]
    gt2 = mod[5:6]
    h = (_rms(x) * (gpre_ref[...] * (1.0 + sc2)) + sh2).astype(BF16)
    y = None
    for c in range(D_FF // D_MODEL):
        cols = slice(c * D_MODEL, (c + 1) * D_MODEL)
        a = jnp.maximum(_dot(h, w1_ref[0, :, cols]), 0.0)
        t = _dot((a * a).astype(BF16), w2_ref[0, cols, :])
        y = t if y is None else y + t
    o_ref[0] = x + gt2 * (_rms(y) * gpost_ref[...])


def _mlp_call(x, mod, gpre, gpost, w1, w2, layer):
    b, s, d = x.shape
    tm = ROW_TILE
    const2 = lambda bi, si: (0, 0)
    lay3 = lambda bi, si: (layer, 0, 0)
    row3 = lambda bi, si: (bi, si, 0)
    return pl.pallas_call(
        _mlp_kernel,
        grid=(b, s // tm),
        in_specs=[
            pl.BlockSpec((1, tm, d), row3),
            pl.BlockSpec((1, 1, N_MOD, d), lambda bi, si: (layer, bi, 0, 0)),
            pl.BlockSpec((1, d), const2),
            pl.BlockSpec((1, d), const2),
            pl.BlockSpec((1, d, D_FF), lay3),
            pl.BlockSpec((1, D_FF, d), lay3),
        ],
        out_specs=pl.BlockSpec((1, tm, d), row3),
        out_shape=jax.ShapeDtypeStruct((b, s, d), F32),
        compiler_params=pltpu.CompilerParams(
            dimension_semantics=("arbitrary", "arbitrary"), vmem_limit_bytes=VMEM_LIMIT),
        name="relu2_mlp",
    )(x, mod, gpre, gpost, w1, w2)


def kernel(x, c, ada_w, ada_b, mix_pre_g, mix_post_g, mlp_pre_g, mlp_post_g, w_in, gmlp_ln_g, gmlp_ln_b, gmlp_ws, gmlp_bs, w_a_out, conv_w, conv_b, conv_ln_g, conv_ln_b, w_b_out, fox_bf, w_c_out, w_out, mlp_w1, mlp_w2):
    n_layers = ada_w.shape[0]
    b, s, d = x.shape
    assert d == D_MODEL and s % ROW_TILE == 0 and s % ATTN_TILE == 0
    assert (ROW_TILE // 2) % CUM_CHUNK == 0 and (ROW_TILE // 2) % CHUNK == 0

    mod = _ada_call(c, ada_w, ada_b).reshape(n_layers, b, N_MOD, d)

    o_gate = 7 * D_MIX + N_HEADS
    w_in_b = w_in.astype(BF16)
    w_gate_b = w_in_b[:, :, o_gate:]
    w_a_b, w_b_b, w_c_b = w_a_out.astype(BF16), w_b_out.astype(BF16), w_c_out.astype(BF16)
    w_out_b, w1_b, w2_b = w_out.astype(BF16), mlp_w1.astype(BF16), mlp_w2.astype(BF16)

    row = lambda a: a.reshape(1, -1)
    for l in range(n_layers):
        bf = jnp.pad(fox_bf[l], (0, LANES - N_HEADS)).reshape(1, LANES)
        wsp = gmlp_ws[l].reshape(N_PAIRS, 2, CHUNK, CHUNK).transpose(0, 2, 1, 3)
        wsp = wsp.reshape(N_PAIRS, CHUNK, 2 * CHUNK).astype(BF16)
        bsp = jnp.repeat(gmlp_bs[l].T, GROUP_DIM, axis=1)
        cw = jnp.pad(conv_w[l], ((0, CONV_HALO - CONV_WIDTH), (0, 0)))

        ya, yb, qkv, frow = _mixer_in_call(
            x, mod, row(mix_pre_g[l]), w_in_b, w_b_b,
            row(gmlp_ln_g[l]), row(gmlp_ln_b[l]), wsp, bsp, cw, row(conv_b[l]),
            row(conv_ln_g[l]), row(conv_ln_b[l]), bf, l)
        yc = _attn_call(qkv, frow)
        x = _merge_call(
            x, mod, row(mix_pre_g[l]), row(mix_post_g[l]), ya, yb, yc, w_gate_b,
            w_a_b, w_c_b, w_out_b, l)
        x = _mlp_call(x, mod, row(mlp_pre_g[l]), row(mlp_post_g[l]), w1_b, w2_b, l)
    return x
```

```python
import math

import jax
import jax.numpy as jnp
from jax import lax
from jax.experimental import pallas as pl
from jax.experimental.pallas import tpu as pltpu

D_MODEL = 1024
D_MIX = 512
N_GROUPS = 8
CHUNK = 128
GROUP_DIM = D_MIX // N_GROUPS
CONV_WIDTH = 31
N_HEADS = 8
HEAD_DIM = D_MIX // N_HEADS
N_BRANCH = 3
D_FF = 4 * D_MODEL
N_MOD = 6
NORM_EPS = 1e-6

LANES = 128
SUBLANES = 8
N_PAIRS = D_MIX // LANES
CONV_HALO = 32
CUM_CHUNK = 256
VMEM_LIMIT = 56 * 1024 * 1024

ROW_TILE = 512
ATTN_TILE = 1024
STAGE_CHUNK = 512
V_ROWS = HEAD_DIM + 16
LOG2E = math.log2(math.e)

F32 = jnp.float32
BF16 = jnp.bfloat16


def _dot(a, b):
    return jnp.dot(a, b, preferred_element_type=F32)


def _sigmoid(x):
    return 0.5 * jnp.tanh(0.5 * x) + 0.5


def _gelu_tanh(x):
    c = math.sqrt(2.0 / math.pi)
    return 0.5 * x * (1.0 + jnp.tanh(c * (x + 0.044715 * (x * x * x))))


def _rms(x):
    return x * lax.rsqrt(jnp.mean(x * x, axis=-1, keepdims=True) + NORM_EPS)


def _layer_norm(x, g, b):
    mu = jnp.mean(x, axis=-1, keepdims=True)
    xc = x - mu
    var = jnp.mean(xc * xc, axis=-1, keepdims=True)
    return xc * lax.rsqrt(var + NORM_EPS) * g + b


def _split3(x):
    hi = x.astype(BF16)
    r = x - hi.astype(F32)
    mid = r.astype(BF16)
    lo = (r - mid.astype(F32)).astype(BF16)
    return hi, mid, lo


def _ada_kernel(c_ref, w_ref, b_ref, o_ref):
    c = c_ref[...]
    c_act = (c * _sigmoid(c)).astype(BF16)
    o_ref[0] = _dot(c_act, w_ref[0].astype(BF16)) + b_ref[0]


def _ada_call(c, ada_w, ada_b):
    n_layers, d, n_out = ada_w.shape
    b = c.shape[0]
    tn = 2 * D_MODEL
    return pl.pallas_call(
        _ada_kernel,
        grid=(n_layers, n_out // tn),
        in_specs=[
            pl.BlockSpec((b, d), lambda l, j: (0, 0)),
            pl.BlockSpec((1, d, tn), lambda l, j: (l, 0, j)),
            pl.BlockSpec((1, 1, tn), lambda l, j: (l, 0, j)),
        ],
        out_specs=pl.BlockSpec((1, b, tn), lambda l, j: (l, 0, j)),
        out_shape=jax.ShapeDtypeStruct((n_layers, b, n_out), F32),
        compiler_params=pltpu.CompilerParams(
            dimension_semantics=("arbitrary", "arbitrary"), vmem_limit_bytes=VMEM_LIMIT),
        name="ada_mod",
    )(c, ada_w, ada_b.reshape(n_layers, 1, n_out))


def _mixer_in_kernel(x_ref, mod_ref, g_ref, w_ref, wb_ref,
                     lng_ref, lnb_ref, wsp_ref, bsp_ref, cw_ref, cb_ref, clg_ref, clb_ref,
                     bf_ref,
                     ya_ref, yb_ref, qkv_ref, frow_ref,
                     uv_ref, f_ref, zs_ref, sh_ref, carry_ref):
    tm = x_ref.shape[1]
    hr = tm // 2
    s_idx = pl.program_id(1)

    @pl.when(s_idx == 0)
    def _():
        zs_ref[0:CONV_HALO, :] = jnp.zeros((CONV_HALO, D_MIX), F32)
        carry_ref[...] = jnp.zeros_like(carry_ref)

    mod = mod_ref[0, 0]
    sh1 = mod[0:1]
    sc1 = mod[1:2]
    scale = LOG2E / math.sqrt(HEAD_DIM)
    first = CONV_HALO - (CONV_WIDTH - 1)

    def project(hf):
        rows = slice(hf * hr, (hf + 1) * hr)
        h = (_rms(x_ref[0, rows, :]) * (g_ref[...] * (1.0 + sc1)) + sh1).astype(BF16)
        uv_ref[hf] = _dot(h, w_ref[0, :, 0:2 * D_MIX])
        glu = _dot(h, w_ref[0, :, 2 * D_MIX:4 * D_MIX])
        zs_ref[CONV_HALO + hf * hr:CONV_HALO + (hf + 1) * hr, :] = (
            glu[:, :D_MIX] * _sigmoid(glu[:, D_MIX:]))
        qkv = _dot(h, w_ref[0, :, 4 * D_MIX:7 * D_MIX])
        for p in range(3 * N_PAIRS):
            blk = qkv[:, p * LANES:(p + 1) * LANES]
            qkv_ref[0, p, rows, :] = (blk * scale if p < N_PAIRS else blk).astype(BF16)
        f_ref[hf] = _dot(h, w_ref[0, :, 7 * D_MIX:7 * D_MIX + LANES]) + bf_ref[...]

    def mix(hf):
        row0 = hf * hr
        uv = _gelu_tanh(uv_ref[hf])
        u_a = uv[:, :D_MIX]
        v_a = _layer_norm(uv[:, D_MIX:], lng_ref[...], lnb_ref[...]).astype(BF16)
        n_chunks = hr // CHUNK
        lane = lax.broadcasted_iota(jnp.int32, (CHUNK, LANES), 1)
        w_row = lax.broadcasted_iota(jnp.int32, (CHUNK, 2 * CHUNK), 0)
        w_col = lax.broadcasted_iota(jnp.int32, (CHUNK, 2 * CHUNK), 1)
        causal_w = (w_col % CHUNK) <= w_row
        zero_bf = jnp.zeros((CHUNK, LANES), BF16)
        for p in range(N_PAIRS):
            cols = slice(p * LANES, (p + 1) * LANES)
            w_pair = jnp.where(causal_w, wsp_ref[p], jnp.zeros_like(wsp_ref[p]))
            rhs = []
            for c in range(n_chunks):
                vc = v_a[c * CHUNK:(c + 1) * CHUNK, cols]
                rhs.append(jnp.concatenate(
                    [jnp.where(lane < GROUP_DIM, vc, zero_bf),
                     jnp.where(lane >= GROUP_DIM, vc, zero_bf)], axis=0))
            sv = _dot(w_pair, jnp.concatenate(rhs, axis=1))
            bias = bsp_ref[:, cols]
            for c in range(n_chunks):
                rows = slice(c * CHUNK, (c + 1) * CHUNK)
                ya_ref[0, row0 + c * CHUNK:row0 + (c + 1) * CHUNK, cols] = (
                    u_a[rows, cols] * (sv[:, c * LANES:(c + 1) * LANES] + bias)).astype(BF16)

        n_rows = hr + CONV_HALO - SUBLANES
        conv = []
        for c in range(N_PAIRS):
            cols = slice(c * LANES, (c + 1) * LANES)
            for r in range(1, SUBLANES):
                sh_ref[c, r - 1, row0:row0 + n_rows, :] = zs_ref[row0 + r:row0 + r + n_rows, cols]
            acc = jnp.broadcast_to(cb_ref[:, cols], (hr, LANES))
            for j in range(CONV_WIDTH):
                r = (first + j) % SUBLANES
                base = row0 + first + j - r
                src = (zs_ref[base:base + hr, cols] if r == 0
                       else sh_ref[c, r - 1, base:base + hr, :])
                acc = acc + cw_ref[j:j + 1, cols] * src
            conv.append(acc)
        zn = _layer_norm(jnp.concatenate(conv, axis=1), clg_ref[...], clb_ref[...])
        yb_ref[0, row0:row0 + hr, :] = _dot((zn * _sigmoid(zn)).astype(BF16), wb_ref[0])

        f_raw = f_ref[hf]
        log_f = jnp.minimum(f_raw, 0.0) - jnp.log1p(jnp.exp(-jnp.abs(f_raw)))
        t_row = lax.broadcasted_iota(jnp.int32, (CUM_CHUNK, CUM_CHUNK), 0)
        t_col = lax.broadcasted_iota(jnp.int32, (CUM_CHUNK, CUM_CHUNK), 1)
        tri = jnp.where(t_col <= t_row, 1.0, 0.0).astype(BF16)
        carry = carry_ref[...]
        for c in range(hr // CUM_CHUNK):
            rows = slice(c * CUM_CHUNK, (c + 1) * CUM_CHUNK)
            hi, mid, lo = _split3(log_f[rows])
            cs = _dot(tri, hi) + _dot(tri, mid) + _dot(tri, lo) + carry
            carry = cs[CUM_CHUNK - 1:CUM_CHUNK, :]
            frow_ref[0, :, row0 + c * CUM_CHUNK:row0 + (c + 1) * CUM_CHUNK] = cs.T[0:N_HEADS, :]
        carry_ref[...] = carry

    project(0)
    project(1)
    mix(0)
    mix(1)
    zs_ref[0:CONV_HALO, :] = zs_ref[tm:tm + CONV_HALO, :]


def _mixer_in_call(x, mod, g, w_in, wb, lng, lnb, wsp, bsp, cw, cb, clg, clb, bf, layer):
    b, s, d = x.shape
    tm = ROW_TILE
    n_cols = 7 * D_MIX + LANES
    const2 = lambda bi, si: (0, 0)
    const3 = lambda bi, si: (0, 0, 0)
    row3 = lambda bi, si: (bi, si, 0)
    out_shapes = (
        jax.ShapeDtypeStruct((b, s, D_MIX), BF16),
        jax.ShapeDtypeStruct((b, s, d), F32),
        jax.ShapeDtypeStruct((b, 3 * N_PAIRS, s, LANES), BF16),
        jax.ShapeDtypeStruct((b, N_HEADS, s), F32),
    )
    return pl.pallas_call(
        _mixer_in_kernel,
        grid=(b, s // tm),
        in_specs=[
            pl.BlockSpec((1, tm, d), row3),
            pl.BlockSpec((1, 1, N_MOD, d), lambda bi, si: (layer, bi, 0, 0)),
            pl.BlockSpec((1, d), const2),
            pl.BlockSpec((1, d, n_cols), lambda bi, si: (layer, 0, 0)),
            pl.BlockSpec((1, D_MIX, d), lambda bi, si: (layer, 0, 0)),
            pl.BlockSpec((1, D_MIX), const2),
            pl.BlockSpec((1, D_MIX), const2),
            pl.BlockSpec((N_PAIRS, CHUNK, 2 * CHUNK), const3),
            pl.BlockSpec((CHUNK, D_MIX), const2),
            pl.BlockSpec((CONV_HALO, D_MIX), const2),
            pl.BlockSpec((1, D_MIX), const2),
            pl.BlockSpec((1, D_MIX), const2),
            pl.BlockSpec((1, D_MIX), const2),
            pl.BlockSpec((1, LANES), const2),
        ],
        out_specs=(
            pl.BlockSpec((1, tm, D_MIX), row3),
            pl.BlockSpec((1, tm, d), row3),
            pl.BlockSpec((1, 3 * N_PAIRS, tm, LANES), lambda bi, si: (bi, 0, si, 0)),
            pl.BlockSpec((1, N_HEADS, tm), lambda bi, si: (bi, 0, si)),
        ),
        out_shape=out_shapes,
        scratch_shapes=[
            pltpu.VMEM((2, tm // 2, 2 * D_MIX), F32),
            pltpu.VMEM((2, tm // 2, LANES), F32),
            pltpu.VMEM((CONV_HALO + tm, D_MIX), F32),
            pltpu.VMEM((N_PAIRS, SUBLANES - 1, tm + CONV_HALO - SUBLANES, LANES), F32),
            pltpu.VMEM((1, LANES), F32),
        ],
        compiler_params=pltpu.CompilerParams(
            dimension_semantics=("arbitrary", "arbitrary"), vmem_limit_bytes=VMEM_LIMIT),
        name="mixer_in",
    )(x, mod, g, w_in, wb, lng, lnb, wsp, bsp, cw, cb, clg, clb, bf)


def _aug_key_rows(f):
    n = f.shape[1]
    hi, mid, lo = _split3(f)
    sub = lax.broadcasted_iota(jnp.int32, (N_HEADS, n), 0)
    ones3 = jnp.where(sub < 3, 1.0, 0.0)
    return jnp.concatenate(
        [-hi.astype(F32), -mid.astype(F32), -lo.astype(F32), jnp.zeros((N_HEADS, n), F32),
         ones3, jnp.zeros((LANES - 5 * N_HEADS, n), F32)], axis=0)


def _aug_query_rows(f, head, row):
    hi, mid, lo = _split3(f)
    ones = (row == head) | (row == head + N_HEADS) | (row == head + 2 * N_HEADS)
    base = jnp.where(ones, 1.0, 0.0)
    return jnp.where(row == 4 * N_HEADS, hi.astype(F32),
                     jnp.where(row == 4 * N_HEADS + 1, mid.astype(F32),
                               jnp.where(row == 4 * N_HEADS + 2, lo.astype(F32), base)))


def _attn_kernel(q_ref, k_ref, v_ref, frow_ref, o_ref,
                 kk_ref, vta_ref, vtb_ref, qq_ref, sta_ref, stb_ref, mba_ref, mbb_ref,
                 m_ref, acca_ref, accb_ref):
    s_len = k_ref.shape[2]
    tq = tk = ATTN_TILE
    pair = pl.program_id(1)

    ones_rows = jnp.ones((V_ROWS - HEAD_DIM, STAGE_CHUNK), BF16)
    for c in range(s_len // STAGE_CHUNK):
        cols = slice(c * STAGE_CHUNK, (c + 1) * STAGE_CHUNK)
        kk_ref[cols, 0:LANES] = k_ref[0, 0, cols, :]
        aug_t = _aug_key_rows(frow_ref[0, :, cols] * LOG2E)
        kk_ref[cols, LANES:2 * LANES] = aug_t.T.astype(BF16)
        vt = v_ref[0, 0, cols, :].astype(F32).T
        for vt_ref, head_rows in ((vta_ref, vt[:HEAD_DIM]), (vtb_ref, vt[HEAD_DIM:])):
            vt_ref[0:HEAD_DIM, cols] = head_rows.astype(BF16)
            vt_ref[HEAD_DIM:V_ROWS, cols] = ones_rows

    n_q = s_len // tq
    row = lax.broadcasted_iota(jnp.int32, (LANES, tq), 0)
    lo_rows = row < HEAD_DIM

    for qi in range(n_q):
        q_cols = slice(qi * tq, (qi + 1) * tq)
        qt = q_ref[0, 0, q_cols, :].astype(F32).T
        top = jnp.concatenate([jnp.where(lo_rows, qt, 0.0), jnp.where(lo_rows, 0.0, qt)], axis=1)
        aug = []
        for hh in range(2):
            head = 2 * pair + hh
            f_q = frow_ref[0, pl.ds(head, 1), q_cols] * LOG2E
            aug.append(_aug_query_rows(f_q, head, row))
        qq_ref[qi] = jnp.concatenate(
            [top, jnp.concatenate(aug, axis=1)], axis=0).astype(BF16)

    half = tk // 2
    acc_refs = (acca_ref, accb_ref)
    vt_refs = (vta_ref, vtb_ref)
    bufs = ((sta_ref, mba_ref), (stb_ref, mbb_ref))

    def produce(qi, j, buf):
        st_ref, mb_ref = buf
        st = _dot(kk_ref[j * tk:(j + 1) * tk, :], qq_ref[qi])
        st_ref[...] = st
        mb_ref[...] = jnp.max(st, axis=0, keepdims=True)

    def produce_diag(qi, j, buf):
        st_ref, mb_ref = buf
        start = j * tk
        r = lax.broadcasted_iota(jnp.int32, (half, 2 * tq), 0)
        c = lax.broadcasted_iota(jnp.int32, (half, 2 * tq), 1)
        st = _dot(kk_ref[start:start + half, :], qq_ref[qi])
        st = jnp.where(r <= jnp.where(c >= tq, c - tq, c), st, -jnp.inf)
        st_ref[0:half, :] = st
        mb_ref[...] = jnp.max(st, axis=0, keepdims=True)
        qq_late = jnp.concatenate([qq_ref[qi, :, half:tq], qq_ref[qi, :, tq + half:2 * tq]], axis=1)
        r = lax.broadcasted_iota(jnp.int32, (half, tq), 0)
        c = lax.broadcasted_iota(jnp.int32, (half, tq), 1)
        st = _dot(kk_ref[start + half:start + tk, :], qq_late)
        st_ref[half:tk, 0:tq] = jnp.where(r <= jnp.where(c >= half, c - half, c), st, -jnp.inf)

    def consume(j, buf):
        st_ref, mb_ref = buf
        m_old = m_ref[...]
        m_new = jnp.maximum(m_old, mb_ref[...])
        alpha = jnp.exp2(m_old - m_new)
        m_ref[...] = m_new
        pt = jnp.exp2(st_ref[...] - m_new).astype(BF16)
        for hh in range(2):
            cols = slice(hh * tq, (hh + 1) * tq)
            acc_refs[hh][...] = (alpha[:, cols] * acc_refs[hh][...]
                                 + _dot(vt_refs[hh][:, j * tk:(j + 1) * tk], pt[:, cols]))

    def consume_diag(j, buf):
        st_ref, mb_ref = buf
        start = j * tk
        m_old = m_ref[...]
        m_new = jnp.maximum(m_old, mb_ref[...])
        alpha = jnp.exp2(m_old - m_new)
        m_ref[...] = m_new
        pt = jnp.exp2(st_ref[0:half, :] - m_new).astype(BF16)
        for hh in range(2):
            cols = slice(hh * tq, (hh + 1) * tq)
            acc_refs[hh][...] = (alpha[:, cols] * acc_refs[hh][...]
                                 + _dot(vt_refs[hh][:, start:start + half], pt[:, cols]))
        for hh in range(2):
            cols = slice(hh * tq + half, (hh + 1) * tq)
            st = st_ref[half:tk, hh * half:(hh + 1) * half]
            m_old = m_ref[:, cols]
            m_new = jnp.maximum(m_old, jnp.max(st, axis=0, keepdims=True))
            alpha = jnp.exp2(m_old - m_new)
            m_ref[:, cols] = m_new
            pt = jnp.exp2(st - m_new).astype(BF16)
            acc_refs[hh][:, half:tq] = (alpha * acc_refs[hh][:, half:tq]
                                        + _dot(vt_refs[hh][:, start + half:start + tk], pt))

    blocks = [(qi, j) for qi in range(n_q) for j in range(qi + 1)]

    def issue(t):
        qi, j = blocks[t]
        (produce_diag if j == qi else produce)(qi, j, bufs[t % 2])

    issue(0)
    for t, (qi, j) in enumerate(blocks):
        if t + 1 < len(blocks):
            issue(t + 1)
        if j == 0:
            m_ref[...] = jnp.full(m_ref.shape, -1e30, F32)
            acca_ref[...] = jnp.zeros_like(acca_ref)
            accb_ref[...] = jnp.zeros_like(accb_ref)
        (consume_diag if j == qi else consume)(j, bufs[t % 2])
        if j == qi:
            ot = jnp.concatenate(
                [acc_ref[0:HEAD_DIM, :] * (1.0 / acc_ref[HEAD_DIM:HEAD_DIM + 1, :])
                 for acc_ref in acc_refs], axis=0)
            o_ref[0, 0, qi * tq:(qi + 1) * tq, :] = ot.T.astype(BF16)


def _attn_call(qkv, frow):
    b, _, s, _ = qkv.shape
    tq = ATTN_TILE
    return pl.pallas_call(
        _attn_kernel,
        grid=(b, N_PAIRS),
        in_specs=[
            pl.BlockSpec((1, 1, s, LANES), lambda bi, p: (bi, p, 0, 0)),
            pl.BlockSpec((1, 1, s, LANES), lambda bi, p: (bi, N_PAIRS + p, 0, 0)),
            pl.BlockSpec((1, 1, s, LANES), lambda bi, p: (bi, 2 * N_PAIRS + p, 0, 0)),
            pl.BlockSpec((1, N_HEADS, s), lambda bi, p: (bi, 0, 0)),
        ],
        out_specs=pl.BlockSpec((1, 1, s, LANES), lambda bi, p: (bi, p, 0, 0)),
        out_shape=jax.ShapeDtypeStruct((b, N_PAIRS, s, LANES), BF16),
        scratch_shapes=[
            pltpu.VMEM((s, 2 * LANES), BF16),
            pltpu.VMEM((V_ROWS, s), BF16),
            pltpu.VMEM((V_ROWS, s), BF16),
            pltpu.VMEM((s // tq, 2 * LANES, 2 * tq), BF16),
            pltpu.VMEM((tq, 2 * tq), F32),
            pltpu.VMEM((tq, 2 * tq), F32),
            pltpu.VMEM((1, 2 * tq), F32),
            pltpu.VMEM((1, 2 * tq), F32),
            pltpu.VMEM((1, 2 * tq), F32),
            pltpu.VMEM((V_ROWS, tq), F32),
            pltpu.VMEM((V_ROWS, tq), F32),
        ],
        compiler_params=pltpu.CompilerParams(
            dimension_semantics=("arbitrary", "arbitrary"), vmem_limit_bytes=VMEM_LIMIT),
        name="fox_attn",
    )(qkv, qkv, qkv, frow)


def _merge_kernel(x_ref, mod_ref, gpre_ref, gpost_ref, ya_ref, yb_ref, yc_ref,
                  wgate_ref, wa_ref, wc_ref, wout_ref, o_ref):
    mod = mod_ref[0, 0]
    sh1 = mod[0:1]
    sc1 = mod[1:2]
    gt1 = mod[2:3]
    hr = x_ref.shape[1] // 2

    for hf in range(2):
        rows = slice(hf * hr, (hf + 1) * hr)
        x = x_ref[0, rows, :]
        h = (_rms(x) * (gpre_ref[...] * (1.0 + sc1)) + sh1).astype(BF16)

        def gate(br):
            return _sigmoid(_dot(h, wgate_ref[0, :, br * D_MODEL:(br + 1) * D_MODEL]))

        yc = jnp.concatenate([yc_ref[0, p, rows, :] for p in range(N_PAIRS)], axis=1)
        merged = (gate(0) * _dot(ya_ref[0, rows, :], wa_ref[0]) + gate(1) * yb_ref[0, rows, :]
                  + gate(2) * _dot(yc, wc_ref[0]))
        y = _dot(merged.astype(BF16), wout_ref[0])
        o_ref[0, rows, :] = x + gt1 * (_rms(y) * gpost_ref[...])


def _merge_call(x, mod, gpre, gpost, ya, yb, yc, wgate, wa, wc, wout, layer):
    b, s, d = x.shape
    tm = ROW_TILE
    const2 = lambda bi, si: (0, 0)
    lay3 = lambda bi, si: (layer, 0, 0)
    row3 = lambda bi, si: (bi, si, 0)
    return pl.pallas_call(
        _merge_kernel,
        grid=(b, s // tm),
        in_specs=[
            pl.BlockSpec((1, tm, d), row3),
            pl.BlockSpec((1, 1, N_MOD, d), lambda bi, si: (layer, bi, 0, 0)),
            pl.BlockSpec((1, d), const2),
            pl.BlockSpec((1, d), const2),
            pl.BlockSpec((1, tm, D_MIX), row3),
            pl.BlockSpec((1, tm, d), row3),
            pl.BlockSpec((1, N_PAIRS, tm, LANES), lambda bi, si: (bi, 0, si, 0)),
            pl.BlockSpec((1, d, N_BRANCH * d), lay3),
            pl.BlockSpec((1, D_MIX, d), lay3),
            pl.BlockSpec((1, D_MIX, d), lay3),
            pl.BlockSpec((1, d, d), lay3),
        ],
        out_specs=pl.BlockSpec((1, tm, d), row3),
        out_shape=jax.ShapeDtypeStruct((b, s, d), F32),
        compiler_params=pltpu.CompilerParams(
            dimension_semantics=("arbitrary", "arbitrary"), vmem_limit_bytes=VMEM_LIMIT),
        name="mixer_merge",
    )(x, mod, gpre, gpost, ya, yb, yc, wgate, wa, wc, wout)


def _mlp_kernel(x_ref, mod_ref, gpre_ref, gpost_ref, w1_ref, w2_ref, o_ref):
    mod = mod_ref[0, 0]
    sh2 = mod[3:4]
    sc2 = mod[4:5]
    gt2 = mod[5:6]
    hr = x_ref.shape[1] // 2
    for hf in range(2):
        rows = slice(hf * hr, (hf + 1) * hr)
        x = x_ref[0, rows, :]
        h = (_rms(x) * (gpre_ref[...] * (1.0 + sc2)) + sh2).astype(BF16)
        y = None
        for c in range(D_FF // D_MODEL):
            cols = slice(c * D_MODEL, (c + 1) * D_MODEL)
            a = jnp.maximum(_dot(h, w1_ref[0, :, cols]), 0.0)
            t = _dot((a * a).astype(BF16), w2_ref[0, cols, :])
            y = t if y is None else y + t
        o_ref[0, rows, :] = x + gt2 * (_rms(y) * gpost_ref[...])


def _mlp_call(x, mod, gpre, gpost, w1, w2, layer):
    b, s, d = x.shape
    tm = ROW_TILE
    const2 = lambda bi, si: (0, 0)
    lay3 = lambda bi, si: (layer, 0, 0)
    row3 = lambda bi, si: (bi, si, 0)
    return pl.pallas_call(
        _mlp_kernel,
        grid=(b, s // tm),
        in_specs=[
            pl.BlockSpec((1, tm, d), row3),
            pl.BlockSpec((1, 1, N_MOD, d), lambda bi, si: (layer, bi, 0, 0)),
            pl.BlockSpec((1, d), const2),
            pl.BlockSpec((1, d), const2),
            pl.BlockSpec((1, d, D_FF), lay3),
            pl.BlockSpec((1, D_FF, d), lay3),
        ],
        out_specs=pl.BlockSpec((1, tm, d), row3),
        out_shape=jax.ShapeDtypeStruct((b, s, d), F32),
        compiler_params=pltpu.CompilerParams(
            dimension_semantics=("arbitrary", "arbitrary"), vmem_limit_bytes=VMEM_LIMIT),
        name="relu2_mlp",
    )(x, mod, gpre, gpost, w1, w2)


def kernel(x, c, ada_w, ada_b, mix_pre_g, mix_post_g, mlp_pre_g, mlp_post_g, w_in, gmlp_ln_g, gmlp_ln_b, gmlp_ws, gmlp_bs, w_a_out, conv_w, conv_b, conv_ln_g, conv_ln_b, w_b_out, fox_bf, w_c_out, w_out, mlp_w1, mlp_w2):
    n_layers = ada_w.shape[0]
    b, s, d = x.shape
    assert d == D_MODEL and s % ROW_TILE == 0 and s % ATTN_TILE == 0
    assert (ROW_TILE // 2) % CUM_CHUNK == 0 and (ROW_TILE // 2) % CHUNK == 0

    mod = _ada_call(c, ada_w, ada_b).reshape(n_layers, b, N_MOD, d)

    o_gate = 7 * D_MIX + N_HEADS
    w_in_b = w_in.astype(BF16)
    w_gate_b = w_in_b[:, :, o_gate:]
    w_a_b, w_b_b, w_c_b = w_a_out.astype(BF16), w_b_out.astype(BF16), w_c_out.astype(BF16)
    w_out_b, w1_b, w2_b = w_out.astype(BF16), mlp_w1.astype(BF16), mlp_w2.astype(BF16)

    row = lambda a: a.reshape(1, -1)
    for l in range(n_layers):
        bf = jnp.pad(fox_bf[l], (0, LANES - N_HEADS)).reshape(1, LANES)
        wsp = gmlp_ws[l].reshape(N_PAIRS, 2, CHUNK, CHUNK).transpose(0, 2, 1, 3)
        wsp = wsp.reshape(N_PAIRS, CHUNK, 2 * CHUNK).astype(BF16)
        bsp = jnp.repeat(gmlp_bs[l].T, GROUP_DIM, axis=1)
        cw = jnp.pad(conv_w[l], ((0, CONV_HALO - CONV_WIDTH), (0, 0)))

        ya, yb, qkv, frow = _mixer_in_call(
            x, mod, row(mix_pre_g[l]), w_in_b, w_b_b,
            row(gmlp_ln_g[l]), row(gmlp_ln_b[l]), wsp, bsp, cw, row(conv_b[l]),
            row(conv_ln_g[l]), row(conv_ln_b[l]), bf, l)
        yc = _attn_call(qkv, frow)
        x = _merge_call(
            x, mod, row(mix_pre_g[l]), row(mix_post_g[l]), ya, yb, yc, w_gate_b,
            w_a_b, w_c_b, w_out_b, l)
        x = _mlp_call(x, mod, row(mlp_pre_g[l]), row(mlp_post_g[l]), w1_b, w2_b, l)
    return x
```
